```python
import math
import jax, jax.numpy as jnp
from jax import lax
import numpy as np

D_MODEL = 1024
BATCH = 8
SEQ = 2048
DEPTH = 4

GRID_W = 64
CTX_LEN = 256
N_MIXERS = 3
N_A = (DEPTH + 2) // 3
N_B = (DEPTH + 1) // 3
N_C = DEPTH // 3

DA_HEADS = 8
DA_HEAD_DIM = D_MODEL // (2 * DA_HEADS)
Q_BLOCK = 128

RET_HEADS = 4
RET_QK_DIM = D_MODEL // RET_HEADS
RET_V_DIM = 2 * RET_QK_DIM
RET_CHUNK = 128

NA_HEADS = 16
NA_HEAD_DIM = D_MODEL // NA_HEADS
NA_KH = 8
NA_KW = 16

FFN_HIDDEN = -(-8 * D_MODEL // (3 * 256)) * 256
ROPE_BASE = 10000.0
EPS = 1e-6
F32 = jnp.float32

kernel_name = 'hybrid_diffattn_retention_natten_prefix_trunk'


def _rms(x):
    x32 = x.astype(F32)
    return x32 * lax.rsqrt(jnp.mean(x32 * x32, axis=-1, keepdims=True) + EPS)


def rms_norm(x, g):
    return (_rms(x) * g.astype(F32)).astype(x.dtype)


def modulate(x, g, shift, scale):
    return rms_norm(x, g) * (1 + scale) + shift


def ada_mod(cond, w, b):
    m = jax.nn.silu(cond) @ w + b
    return [a[:, None, :] for a in jnp.split(m, 6, axis=-1)]


def axial_rope(x, rows, cols):
    d = x.shape[-1]
    f = d // 4
    inv = ROPE_BASE ** (-jnp.arange(f, dtype=F32) / f)
    ang = jnp.concatenate([rows.astype(F32)[:, None] * inv, cols.astype(F32)[:, None] * inv], axis=-1)[:, None, :]
    cos, sin = jnp.cos(ang), jnp.sin(ang)
    x32 = x.astype(F32)
    x1, x2 = x32[..., : d // 2], x32[..., d // 2:]
    return jnp.concatenate([x1 * cos - x2 * sin, x1 * sin + x2 * cos], axis=-1).astype(x.dtype)


def swiglu(h, w_in, w_out):
    a, b = jnp.split(h @ w_in, 2, axis=-1)
    return (jax.nn.silu(a) * b) @ w_out


def _diff_core(q, k, v, lam, scale):
    s = jnp.einsum('bqnd,bknd->bnqk', q, k).astype(F32) * scale
    p = jax.nn.softmax(s, axis=-1)
    b, n, tq, tk = p.shape
    p = p.reshape(b, n // 2, 2, tq, tk)
    a = p[:, :, 0] - lam * p[:, :, 1]
    return jnp.einsum('bhqk,bkhe->bqhe', a.astype(v.dtype), v)


def diff_attention(hx, hz, w_qkv, w_o, lam_p, subln_g, lam_init, rows, cols, need_ctx):
    B, L, D = hx.shape
    H, dh = DA_HEADS, DA_HEAD_DIM

    def qkv(h):
        T = h.shape[1]
        q, k, v = jnp.split(h @ w_qkv, 3, axis=-1)
        return q.reshape(B, T, 2 * H, dh), k.reshape(B, T, 2 * H, dh), v.reshape(B, T, H, 2 * dh)

    qx, kx, vx = qkv(hx)
    qz, kz, vz = qkv(hz)
    qx = axial_rope(qx, rows, cols)
    kx = axial_rope(kx, rows, cols)
    lp = lam_p.astype(F32)
    lam = jnp.exp(jnp.sum(lp[0] * lp[1])) - jnp.exp(jnp.sum(lp[2] * lp[3])) + lam_init
    scale = dh ** -0.5
    k_all = jnp.concatenate([kx, kz], axis=1)
    v_all = jnp.concatenate([vx, vz], axis=1)
    nb = L // Q_BLOCK
    qb = jnp.moveaxis(qx.reshape(B, nb, Q_BLOCK, 2 * H, dh), 1, 0)
    ox = lax.map(lambda qi: _diff_core(qi, k_all, v_all, lam, scale), qb)
    ox = jnp.moveaxis(ox, 0, 1).reshape(B, L, H, 2 * dh)

    def finish(o):
        o = rms_norm(o, subln_g) * (1.0 - lam_init)
        return o.reshape(B, o.shape[1], H * 2 * dh) @ w_o

    out_x = finish(ox)
    out_z = finish(_diff_core(qz, kz, vz, lam, scale)) if need_ctx else None
    return out_x, out_z


def _retention_scan(q, k, v, log_g, S0, inclusive):
    B, T, H, dk = q.shape
    dv = v.shape[-1]
    C = RET_CHUNK
    n = T // C

    def chunks(a):
        return jnp.moveaxis(a.astype(F32).reshape(B, n, C, H, a.shape[-1]), 1, 0)

    idx = jnp.arange(C, dtype=F32)
    diff = idx[:, None] - idx[None, :]
    mask = (diff >= 0) if inclusive else (diff > 0)
    dmat = jnp.where(mask[None], jnp.exp(jnp.maximum(diff, 0.0)[None] * log_g[:, None, None]), 0.0)
    xi = jnp.exp((idx[:, None] + 1.0) * log_g[None, :])
    zeta = jnp.exp((C - 1.0 - idx)[:, None] * log_g[None, :])
    decay_c = jnp.exp(C * log_g)

    def step(S, inp):
        qc, kc, vc = inp
        s = jnp.einsum('bihd,bjhd->bhij', qc, kc) * dmat
        o = jnp.einsum('bhij,bjhe->bihe', s, vc) + jnp.einsum('bihd,bhde->bihe', qc, S) * xi[None, :, :, None]
        S = S * decay_c[None, :, None, None] + jnp.einsum('bjhd,bjhe->bhde', kc * zeta[None, :, :, None], vc)
        return S, o

    S, o = lax.scan(step, S0, (chunks(q), chunks(k), chunks(v)))
    return jnp.moveaxis(o, 0, 1).reshape(B, T, H, dv), S


def retention(hx, hz, w_in, w_o, decay_logit, rows, cols, need_ctx):
    B = hx.shape[0]
    H, dk, dv = RET_HEADS, RET_QK_DIM, RET_V_DIM

    def proj(h):
        T = h.shape[1]
        q, k, v, g = jnp.split(h @ w_in, [H * dk, 2 * H * dk, 2 * H * dk + H * dv], axis=-1)
        return q.reshape(B, T, H, dk), k.reshape(B, T, H, dk) * dk ** -0.5, v.reshape(B, T, H, dv), g

    qx, kx, vx, gx = proj(hx)
    qz, kz, vz, gz = proj(hz)
    qx = axial_rope(qx, rows, cols)
    kx = axial_rope(kx, rows, cols)
    log_g = jax.nn.log_sigmoid(decay_logit.astype(F32))
    S0 = jnp.zeros((B, H, dk, dv), F32)

    def flip(a):
        return a[:, ::-1]

    oz_f, Sz_f = _retention_scan(qz, kz, vz, log_g[0], S0, True)
    ox_f, _ = _retention_scan(qx, kx, vx, log_g[0], Sz_f, True)
    oz_b, Sz_b = _retention_scan(flip(qz), flip(kz), flip(vz), log_g[1], S0, False)
    ox_b, _ = _retention_scan(flip(qx), flip(kx), flip(vx), log_g[1], Sz_b, False)

    def finish(o, g):
        T = o.shape[1]
        o = _rms(o).reshape(B, T, H * dv).astype(g.dtype)
        return (jax.nn.silu(g) * o) @ w_o

    out_x = finish(ox_f + flip(ox_b), gx)
    out_z = finish(oz_f + flip(oz_b), gz) if need_ctx else None
    return out_x, out_z


def neighborhood_attention(hx, hz, w_qkv, w_o, rpb, need_ctx):
    B, L, D = hx.shape
    H, dh = NA_HEADS, NA_HEAD_DIM
    R = L // GRID_W
    W = GRID_W
    kh = min(NA_KH, R)
    kw = NA_KW
    scale = dh ** -0.5

    def qkv(h):
        T = h.shape[1]
        q, k, v = jnp.split(h @ w_qkv, 3, axis=-1)
        return q.reshape(B, T, H, dh), k.reshape(B, T, H, dh), v.reshape(B, T, H, dh)

    qx, kx, vx = qkv(hx)
    qz, kz, vz = qkv(hz)
    kg = kx.reshape(B, R, W, H, dh)
    vg = vx.reshape(B, R, W, H, dh)
    col = jnp.arange(W)
    cs = jnp.clip(col - kw // 2, 0, W - kw)
    col_ok = (col[None, :] >= cs[:, None]) & (col[None, :] < cs[:, None] + kw)
    dc_idx = jnp.clip(col[None, :] - col[:, None] + NA_KW - 1, 0, 2 * NA_KW - 2)
    rpb32 = rpb.astype(F32)

    def row_block(args):
        r, qr = args
        rs = jnp.clip(r - kh // 2, 0, R - kh)
        kb = lax.dynamic_slice_in_dim(kg, rs, kh, axis=1)
        vb = lax.dynamic_slice_in_dim(vg, rs, kh, axis=1)
        dr_idx = rs + jnp.arange(kh) - r + NA_KH - 1
        bias = rpb32[:, dr_idx[:, None, None], dc_idx[None, :, :]]
        bias = jnp.where(col_ok[None, None], bias, -jnp.inf).transpose(0, 2, 1, 3)
        s_lat = jnp.einsum('bqhd,bkwhd->bhqkw', qr, kb).astype(F32) * scale + bias
        s_ctx = jnp.einsum('bqhd,bnhd->bhqn', qr, kz).astype(F32) * scale
        s = jnp.concatenate([s_lat.reshape(B, H, W, kh * W), s_ctx], axis=-1)
        p = jax.nn.softmax(s, axis=-1).astype(vb.dtype)
        p_lat = p[..., : kh * W].reshape(B, H, W, kh, W)
        return (jnp.einsum('bhqkw,bkwhd->bqhd', p_lat, vb)
                + jnp.einsum('bhqn,bnhd->bqhd', p[..., kh * W:], vz))

    qg = jnp.moveaxis(qx.reshape(B, R, W, H, dh), 1, 0)
    ox = lax.map(row_block, (jnp.arange(R), qg))
    out_x = jnp.moveaxis(ox, 0, 1).reshape(B, L, D) @ w_o
    out_z = None
    if need_ctx:
        s = jnp.einsum('bqhd,bkhd->bhqk', qz, kz).astype(F32) * scale
        p = jax.nn.softmax(s, axis=-1).astype(vz.dtype)
        oz = jnp.einsum('bhqk,bkhd->bqhd', p, vz)
        out_z = oz.reshape(B, oz.shape[1], D) @ w_o
    return out_x, out_z


def setup_inputs(seed: int = 0) -> dict:
    key = jax.random.key(seed)
    ks = jax.random.split(key, 24)
    D = D_MODEL

    def nrm(k, shape, s):
        return jax.random.normal(k, shape, F32) * s

    gamma0 = 1.0 - 2.0 ** (-5.0 - jnp.arange(RET_HEADS, dtype=F32))
    decay_base = jnp.log(gamma0) - jnp.log1p(-gamma0)
    ret_in_w = 2 * RET_HEADS * RET_QK_DIM + 2 * RET_HEADS * RET_V_DIM
    return {
        'x': nrm(ks[0], (BATCH, SEQ, D), 1.0),
        'c': nrm(ks[1], (BATCH, D), 1.0),
        'ctx': nrm(ks[2], (BATCH, CTX_LEN, D), 1.0),
        'c_ctx': nrm(ks[3], (D,), 1.0),
        'w_ada': nrm(ks[4], (DEPTH, D, 6 * D), 0.5 * D ** -0.5),
        'b_ada': nrm(ks[5], (DEPTH, 6 * D), 0.02),
        'norm_g': 1.0 + nrm(ks[6], (DEPTH, 2, D), 0.02),
        'ffn_in': nrm(ks[7], (DEPTH, D, 2 * FFN_HIDDEN), D ** -0.5),
        'ffn_out': nrm(ks[8], (DEPTH, FFN_HIDDEN, D), FFN_HIDDEN ** -0.5),
        'final_g': 1.0 + nrm(ks[9], (D,), 0.02),
        'da_w_qkv': nrm(ks[10], (N_A, D, 3 * D), D ** -0.5),
        'da_w_o': nrm(ks[11], (N_A, D, D), D ** -0.5),
        'da_lambda': nrm(ks[12], (N_A, 4, DA_HEAD_DIM), 0.1),
        'da_subln': 1.0 + nrm(ks[13], (N_A, 2 * DA_HEAD_DIM), 0.02),
        'ret_w_in': nrm(ks[14], (N_B, D, ret_in_w), D ** -0.5),
        'ret_w_o': nrm(ks[15], (N_B, RET_HEADS * RET_V_DIM, D), (RET_HEADS * RET_V_DIM) ** -0.5),
        'ret_decay': decay_base[None, None, :] + nrm(ks[16], (N_B, 2, RET_HEADS), 0.1),
        'na_w_qkv': nrm(ks[17], (N_C, D, 3 * D), D ** -0.5),
        'na_w_o': nrm(ks[18], (N_C, D, D), D ** -0.5),
        'na_rpb': nrm(ks[19], (N_C, NA_HEADS, 2 * NA_KH - 1, 2 * NA_KW - 1), 0.02),
    }


def reference(x, c, ctx, c_ctx, w_ada, b_ada, norm_g, ffn_in, ffn_out, final_g,
              da_w_qkv, da_w_o, da_lambda, da_subln,
              ret_w_in, ret_w_o, ret_decay,
              na_w_qkv, na_w_o, na_rpb):
    B, L, D = x.shape
    t = jnp.arange(L)
    rows = t // GRID_W
    cols = t % GRID_W
    z = ctx
    cz = c_ctx[None, :]
    ia = ib = ic = 0
    for li in range(DEPTH):
        kind = li % N_MIXERS
        need_ctx = li < DEPTH - 1
        sh1, sc1, gt1, sh2, sc2, gt2 = ada_mod(c, w_ada[li], b_ada[li])
        zsh1, zsc1, zgt1, zsh2, zsc2, zgt2 = ada_mod(cz, w_ada[li], b_ada[li])
        hx = modulate(x, norm_g[li, 0], sh1, sc1)
        hz = modulate(z, norm_g[li, 0], zsh1, zsc1)
        if kind == 0:
            lam_init = 0.8 - 0.6 * math.exp(-0.3 * li)
            ox, oz = diff_attention(hx, hz, da_w_qkv[ia], da_w_o[ia], da_lambda[ia], da_subln[ia],
                                    lam_init, rows, cols, need_ctx)
            ia += 1
        elif kind == 1:
            ox, oz = retention(hx, hz, ret_w_in[ib], ret_w_o[ib], ret_decay[ib], rows, cols, need_ctx)
            ib += 1
        else:
            ox, oz = neighborhood_attention(hx, hz, na_w_qkv[ic], na_w_o[ic], na_rpb[ic], need_ctx)
            ic += 1
        x = x + gt1 * ox
        x = x + gt2 * swiglu(modulate(x, norm_g[li, 1], sh2, sc2), ffn_in[li], ffn_out[li])
        if need_ctx:
            z = z + zgt1 * oz
            z = z + zgt2 * swiglu(modulate(z, norm_g[li, 1], zsh2, zsc2), ffn_in[li], ffn_out[li])
    return rms_norm(x, final_g)
```

```python
import functools
import math

import jax
import jax.numpy as jnp
from jax import lax
from jax.experimental import pallas as pl
from jax.experimental.pallas import tpu as pltpu

F32 = jnp.float32
BF16 = jnp.bfloat16

DEPTH = 4
GRID_W = 64
DA_HEADS = 8
RET_HEADS = 4
NA_HEADS = 16
NA_KH = 8
NA_KW = 16
ROPE_BASE = 10000.0
EPS = 1e-6
LANES = 128
RET_CHUNK = 256
MOD_ROWS = 16
VMEM_LIMIT = 56 * 1024 * 1024


def _cparams(n_axes):
    return pltpu.CompilerParams(dimension_semantics=("parallel",) * n_axes, vmem_limit_bytes=VMEM_LIMIT)


def _resident(shape):
    nd = len(shape)
    return pl.BlockSpec(shape, lambda *_: (0,) * nd, pipeline_mode=pl.Buffered(1))


def _rms(x):
    return x * lax.rsqrt(jnp.mean(x * x, axis=-1, keepdims=True) + EPS)


def _modulate(x, g, shift, scale):
    return _rms(x) * g * (1.0 + scale) + shift


def _silu(x):
    return x * jax.nn.sigmoid(x)


def _dot(a, b):
    return jnp.dot(a, b, preferred_element_type=F32)


def _dot_nt(a, b):
    return lax.dot_general(a, b, (((1,), (1,)), ((), ())), preferred_element_type=F32)


def _dot_tn(a, b):
    return lax.dot_general(a, b, (((0,), (0,)), ((), ())), preferred_element_type=F32)


def _ada_kernel(c_ref, w_ref, b_ref, o_ref):
    a = _silu(c_ref[...]).astype(BF16)
    o_ref[...] = _dot(a, w_ref[...].astype(BF16)) + b_ref[...]


def _ada_call(cond, w_ada, b_ada):
    depth, d, d6 = w_ada.shape
    tn = 1536
    return pl.pallas_call(
        _ada_kernel,
        grid=(depth, d6 // tn),
        in_specs=[
            pl.BlockSpec((MOD_ROWS, d), lambda l, j: (0, 0)),
            pl.BlockSpec((None, d, tn), lambda l, j: (l, 0, j)),
            pl.BlockSpec((None, 1, tn), lambda l, j: (l, 0, j)),
        ],
        out_specs=pl.BlockSpec((None, MOD_ROWS, tn), lambda l, j: (l, 0, j)),
        out_shape=jax.ShapeDtypeStruct((depth, MOD_ROWS, d6), F32),
        compiler_params=_cparams(2),
        name="ada_mod",
    )(cond, w_ada, b_ada.reshape(depth, 1, d6))


def _rope64_store(y, cos, sin, first_half, mul, dst):
    for j in range(y.shape[1] // LANES):
        yj = y[:, j * LANES:(j + 1) * LANES]
        sw = jnp.where(first_half, pltpu.roll(yj, LANES - 32, 1), pltpu.roll(yj, 32, 1))
        dst[:, j * LANES:(j + 1) * LANES] = ((yj * cos + sw * sin) * mul).astype(BF16)


def _pre_da_kernel(s_ref, mod_ref, g_ref, w_ref, cos_ref, sin_ref, q_ref, k_ref, v_ref):
    d = s_ref.shape[1]
    h = _modulate(s_ref[...], g_ref[...], mod_ref[0:1, :], mod_ref[1:2, :]).astype(BF16)
    cos = cos_ref[...]
    sin = sin_ref[...]
    first_half = (lax.broadcasted_iota(jnp.int32, cos.shape, 1) % 64) < 32
    _rope64_store(_dot(h, w_ref[:, 0:d]), cos, sin, first_half, 0.125, q_ref)
    _rope64_store(_dot(h, w_ref[:, d:2 * d]), cos, sin, first_half, 1.0, k_ref)
    v_ref[...] = _dot(h, w_ref[:, 2 * d:3 * d]).astype(BF16)


def _pre_na_kernel(s_ref, mod_ref, g_ref, w_ref, q_ref, k_ref, v_ref):
    d = s_ref.shape[1]
    h = _modulate(s_ref[...], g_ref[...], mod_ref[0:1, :], mod_ref[1:2, :]).astype(BF16)
    q_ref[...] = (_dot(h, w_ref[:, 0:d]) * 0.125).astype(BF16)
    k_ref[...] = _dot(h, w_ref[:, d:2 * d]).astype(BF16)
    v_ref[...] = _dot(h, w_ref[:, 2 * d:3 * d]).astype(BF16)


def _pre_ret_kernel(s_ref, mod_ref, g_ref, w_ref, cos_ref, sin_ref, q_ref, k_ref, v_ref, sg_ref):
    d = s_ref.shape[1]
    h = _modulate(s_ref[...], g_ref[...], mod_ref[0:1, :], mod_ref[1:2, :]).astype(BF16)
    cos = cos_ref[...]
    sin = sin_ref[...]
    dk = d // RET_HEADS
    for dst, col0, mul in ((q_ref, 0, 1.0), (k_ref, d, dk ** -0.5)):
        y = _dot(h, w_ref[:, col0:col0 + d])
        for hh in range(RET_HEADS):
            a0 = hh * dk
            x1 = y[:, a0:a0 + LANES]
            x2 = y[:, a0 + LANES:a0 + 2 * LANES]
            dst[:, a0:a0 + LANES] = ((x1 * cos - x2 * sin) * mul).astype(BF16)
            dst[:, a0 + LANES:a0 + 2 * LANES] = ((x2 * cos + x1 * sin) * mul).astype(BF16)
    v_ref[...] = _dot(h, w_ref[:, 2 * d:4 * d]).astype(BF16)
    sg_ref[...] = _silu(_dot(h, w_ref[:, 4 * d:6 * d])).astype(BF16)


def _pre_call(kernel_fn, stream, mods, norm_g2, li, w, rope, out_widths, n_lat, lat_len, tm, name):
    n, d = stream.shape
    n_lat_tiles = n_lat // tm
    tiles_per_batch = lat_len // tm
    n_batch = n_lat // lat_len

    def mod_map(i):
        return (li, jnp.where(i < n_lat_tiles, i // tiles_per_batch, n_batch), 0, 0)

    def rope_map(i):
        return (jnp.where(i < n_lat_tiles, i % tiles_per_batch, tiles_per_batch), 0)

    in_specs = [
        pl.BlockSpec((tm, d), lambda i: (i, 0)),
        pl.BlockSpec((None, None, 6, d), mod_map),
        pl.BlockSpec((None, 1, d), lambda i: (2 * li, 0, 0)),
        _resident(w.shape),
    ]
    args = [stream, mods, norm_g2, w]
    if rope is not None:
        in_specs += [pl.BlockSpec((tm, LANES), rope_map)] * 2
        args += list(rope)
    return pl.pallas_call(
        kernel_fn,
        grid=(n // tm,),
        in_specs=in_specs,
        out_specs=[pl.BlockSpec((tm, wd), lambda i: (i, 0)) for wd in out_widths],
        out_shape=[jax.ShapeDtypeStruct((n, wd), BF16) for wd in out_widths],
        compiler_params=_cparams(1),
        name=name,
    )(*args)


def _post_kernel(a_ref, s_ref, mod_ref, g_ref, wo_ref, win_ref, wout_ref, *rest, final):
    o_ref = rest[-1]
    fh = wout_ref.shape[0]
    x = s_ref[...] + mod_ref[2:3, :] * _dot(a_ref[...], wo_ref[...])
    h = _modulate(x, g_ref[...], mod_ref[3:4, :], mod_ref[4:5, :]).astype(BF16)
    hm = _dot(h, win_ref[...])
    act = (_silu(hm[:, :fh]) * hm[:, fh:]).astype(BF16)
    x = x + mod_ref[5:6, :] * _dot(act, wout_ref[...])
    if final:
        o_ref[...] = _rms(x) * rest[0][...]
    else:
        o_ref[...] = x


def _post_call(a, stream, mods, norm_g2, li, w_o, w_in, w_out, final_g, n_lat, lat_len, tm, final):
    n, d = stream.shape
    n_rows = n_lat if final else n
    n_lat_tiles = n_lat // tm
    tiles_per_batch = lat_len // tm
    n_batch = n_lat // lat_len

    def mod_map(i):
        return (li, jnp.where(i < n_lat_tiles, i // tiles_per_batch, n_batch), 0, 0)

    in_specs = [
        pl.BlockSpec((tm, a.shape[1]), lambda i: (i, 0)),
        pl.BlockSpec((tm, d), lambda i: (i, 0)),
        pl.BlockSpec((None, None, 6, d), mod_map),
        pl.BlockSpec((None, 1, d), lambda i: (2 * li + 1, 0, 0)),
        _resident(w_o.shape),
        _resident(w_in.shape),
        _resident(w_out.shape),
    ]
    args = [a, stream, mods, norm_g2, w_o, w_in, w_out]
    if final:
        in_specs.append(pl.BlockSpec((1, d), lambda i: (0, 0)))
        args.append(final_g.reshape(1, d))
    return pl.pallas_call(
        functools.partial(_post_kernel, final=final),
        grid=(n_rows // tm,),
        in_specs=in_specs,
        out_specs=pl.BlockSpec((tm, d), lambda i: (i, 0)),
        out_shape=jax.ShapeDtypeStruct((n_rows, d), F32),
        compiler_params=_cparams(1),
        name="post_final" if final else "post",
    )(*args)


def _da_kernel(lam_ref, sub_ref, q_ref, kx_ref, kz_ref, vx_ref, vz_ref, o_ref, *, lam_init, n_lat_tiles):
    t = pl.program_id(2)
    lp = lam_ref[...]
    lam = (jnp.exp(jnp.sum(lp[0:1, :] * lp[1:2, :], axis=-1, keepdims=True))
           - jnp.exp(jnp.sum(lp[2:3, :] * lp[3:4, :], axis=-1, keepdims=True)) + lam_init)
    q = q_ref[...].astype(F32)
    lane = lax.broadcasted_iota(jnp.int32, q.shape, 1)
    q_maps = (jnp.where(lane < 64, q, 0.0).astype(BF16), jnp.where(lane >= 64, q, 0.0).astype(BF16))

    def attend(keys, values):
        probs = []
        for qm in q_maps:
            ss = [_dot_nt(qm, k) for k in keys]
            m = ss[0].max(axis=-1, keepdims=True)
            for s in ss[1:]:
                m = jnp.maximum(m, s.max(axis=-1, keepdims=True))
            ps = [jnp.exp(s - m) for s in ss]
            l = ps[0].sum(axis=-1, keepdims=True)
            for p in ps[1:]:
                l = l + p.sum(axis=-1, keepdims=True)
            probs.append((ps, l))
        (p1, l1), (p2, l2) = probs
        c1 = 1.0 / l1
        c2 = lam / l2
        acc = None
        for pa, pb, v in zip(p1, p2, values):
            part = _dot((pa * c1 - pb * c2).astype(BF16), v)
            acc = part if acc is None else acc + part
        o_ref[...] = (_rms(acc) * sub_ref[...] * (1.0 - lam_init)).astype(BF16)

    @pl.when(t < n_lat_tiles)
    def _():
        attend((kx_ref[...], kz_ref[...]), (vx_ref[...], vz_ref[...]))

    @pl.when(t >= n_lat_tiles)
    def _():
        attend((kz_ref[...],), (vz_ref[...],))


def _da_call(q, k, v, lam_p, subln, lam_init, n_batch, lat_len, ctx_len, need_ctx):
    n, d = q.shape
    tq = ctx_len
    n_lat_tiles = lat_len // tq
    ctx_blk0 = n_batch * lat_len // ctx_len

    def q_map(b, h, t):
        return (jnp.where(t < n_lat_tiles, b * n_lat_tiles + t, ctx_blk0 + b), h)

    return pl.pallas_call(
        functools.partial(_da_kernel, lam_init=lam_init, n_lat_tiles=n_lat_tiles),
        grid=(n_batch, DA_HEADS, n_lat_tiles + (1 if need_ctx else 0)),
        in_specs=[
            pl.BlockSpec(lam_p.shape, lambda b, h, t: (0, 0)),
            pl.BlockSpec((1, LANES), lambda b, h, t: (0, 0)),
            pl.BlockSpec((tq, LANES), q_map),
            pl.BlockSpec((lat_len, LANES), lambda b, h, t: (b, h)),
            pl.BlockSpec((ctx_len, LANES), lambda b, h, t: (ctx_blk0 + b, h)),
            pl.BlockSpec((lat_len, LANES), lambda b, h, t: (b, h)),
            pl.BlockSpec((ctx_len, LANES), lambda b, h, t: (ctx_blk0 + b, h)),
        ],
        out_specs=pl.BlockSpec((tq, LANES), q_map),
        out_shape=jax.ShapeDtypeStruct((n if need_ctx else n_batch * lat_len, d), BF16),
        compiler_params=_cparams(3),
        name="diff_attn",
    )(lam_p, subln.reshape(1, LANES), q, k, k, v, v)


def _ret_kernel(dec_ref, qx_ref, qz_ref, kx_ref, kz_ref, vx_ref, vz_ref, gx_ref, gz_ref, ox_ref, oz_ref,
                acc_ref, dsb_ref):
    c = RET_CHUNK
    n_chunks = qx_ref.shape[0] // c
    head = pl.program_id(1)
    lgs = jax.nn.log_sigmoid(dec_ref[...])
    sel = lax.broadcasted_iota(jnp.int32, lgs.shape, 1) == head
    lgs = jnp.sum(jnp.where(sel, lgs, 0.0), axis=-1, keepdims=True)
    lg_f = lgs[0:1, :]
    lg_b = lgs[1:2, :]

    ii = lax.broadcasted_iota(jnp.int32, (c, c), 0)
    jj = lax.broadcasted_iota(jnp.int32, (c, c), 1)
    diff = (ii - jj).astype(F32)
    dmat = jnp.where(diff >= 0.0, jnp.exp(jnp.maximum(diff, 0.0) * lg_f), jnp.exp(jnp.maximum(-diff, 0.0) * lg_b))
    idx = lax.broadcasted_iota(jnp.int32, (c, 1), 0).astype(F32)
    xi_f = jnp.exp((idx + 1.0) * lg_f)
    xi_b = jnp.exp((c - idx) * lg_b)
    zeta_f = jnp.exp((c - 1.0 - idx) * lg_f)
    zeta_b = jnp.exp(idx * lg_b)
    decay_f = jnp.exp(c * lg_f)
    decay_b = jnp.exp(c * lg_b)

    def intra(q, k, v):
        return _dot((_dot_nt(q, k) * dmat).astype(BF16), v)

    def state_updates(k, v):
        k32 = k.astype(F32)
        return _dot_tn((k32 * zeta_f).astype(BF16), v), _dot_tn((k32 * zeta_b).astype(BF16), v)

    def finish(o, gate):
        return (_rms(o) * gate.astype(F32)).astype(BF16)

    qz, kz, vz = qz_ref[...], kz_ref[...], vz_ref[...]
    oz_ref[...] = finish(intra(qz, kz, vz), gz_ref[...])
    s_f, s_b = state_updates(kz, vz)

    for i in range(n_chunks):
        rows = slice(i * c, (i + 1) * c)
        q, k, v = qx_ref[rows, :], kx_ref[rows, :], vx_ref[rows, :]
        acc_ref[rows, :] = intra(q, k, v) + xi_f * _dot(q, s_f.astype(BF16))
        d_f, d_b = state_updates(k, v)
        dsb_ref[i] = d_b
        s_f = decay_f * s_f + d_f

    for i in reversed(range(n_chunks)):
        rows = slice(i * c, (i + 1) * c)
        o = acc_ref[rows, :] + xi_b * _dot(qx_ref[rows, :], s_b.astype(BF16))
        ox_ref[rows, :] = finish(o, gx_ref[rows, :])
        s_b = decay_b * s_b + dsb_ref[i]


def _ret_call(q, k, v, sg, decay, n_batch, lat_len, ctx_len):
    n, dq = q.shape
    dv = v.shape[1]
    dkh = dq // RET_HEADS
    dvh = dv // RET_HEADS
    ctx_blk0 = n_batch * lat_len // ctx_len

    def lat(b, h):
        return (b, h)

    def ctx(b, h):
        return (ctx_blk0 + b, h)

    ox, oz = pl.pallas_call(
        _ret_kernel,
        grid=(n_batch, RET_HEADS),
        in_specs=[
            pl.BlockSpec(decay.shape, lambda b, h: (0, 0)),
            pl.BlockSpec((lat_len, dkh), lat), pl.BlockSpec((ctx_len, dkh), ctx),
            pl.BlockSpec((lat_len, dkh), lat), pl.BlockSpec((ctx_len, dkh), ctx),
            pl.BlockSpec((lat_len, dvh), lat), pl.BlockSpec((ctx_len, dvh), ctx),
            pl.BlockSpec((lat_len, dvh), lat), pl.BlockSpec((ctx_len, dvh), ctx),
        ],
        out_specs=[pl.BlockSpec((lat_len, dvh), lat), pl.BlockSpec((ctx_len, dvh), lambda b, h: (b, h))],
        out_shape=[jax.ShapeDtypeStruct((n_batch * lat_len, dv), BF16),
                   jax.ShapeDtypeStruct((n_batch * ctx_len, dv), BF16)],
        scratch_shapes=[pltpu.VMEM((lat_len, dvh), F32),
                        pltpu.VMEM((lat_len // RET_CHUNK, dkh, dvh), F32)],
        compiler_params=_cparams(2),
        name="retention",
    )(decay, q, q, k, k, v, v, sg, sg)
    return jnp.concatenate([ox, oz], axis=0)


def _na_bias_table(rpb):
    col = jnp.arange(GRID_W)
    cs = jnp.clip(col - NA_KW // 2, 0, GRID_W - NA_KW)
    ok = (col[None, :] >= cs[:, None]) & (col[None, :] < cs[:, None] + NA_KW)
    dc = jnp.clip(col[None, :] - col[:, None] + NA_KW - 1, 0, 2 * NA_KW - 2)
    t = jnp.where(ok[None, None], rpb.astype(F32)[:, :, dc], -jnp.inf)
    return jnp.concatenate([t, t], axis=-1)


def _na_kernel(tb_ref, q_ref, kx_ref, kz_ref, vx_ref, vz_ref, o_ref, *, n_lat_tiles, n_rows):
    t = pl.program_id(2)
    w = GRID_W
    win = NA_KH * w
    lane_q = lax.broadcasted_iota(jnp.int32, (w, LANES), 1)

    def head_select(x, hh):
        lane = lax.broadcasted_iota(jnp.int32, x.shape, 1)
        keep = (lane < 64) if hh == 0 else (lane >= 64)
        return jnp.where(keep, x.astype(F32), 0.0).astype(BF16)

    def softmax_pv(scores, values):
        m = scores[0].max(axis=-1, keepdims=True)
        for s in scores[1:]:
            m = jnp.maximum(m, s.max(axis=-1, keepdims=True))
        ps = [jnp.exp(s - m) for s in scores]
        l = ps[0].sum(axis=-1, keepdims=True)
        for p in ps[1:]:
            l = l + p.sum(axis=-1, keepdims=True)
        acc = _dot(ps[0].astype(BF16), values[0])
        for p, v in zip(ps[1:], values[1:]):
            acc = acc + _dot(p.astype(BF16), v)
        return acc * (1.0 / l)

    @pl.when(t < n_lat_tiles)
    def _():
        kz = kz_ref[...]
        vz = vz_ref[...]
        rows_per_tile = q_ref.shape[0] // w
        for rr in range(rows_per_tile):
            r = t * rows_per_tile + rr
            rs = jnp.clip(r - NA_KH // 2, 0, n_rows - NA_KH)
            oi = r - rs
            start = pl.multiple_of(rs * w, w)
            kwin = kx_ref[pl.ds(start, win), :]
            vwin = vx_ref[pl.ds(start, win), :]
            q = q_ref[rr * w:(rr + 1) * w, :]
            outs = []
            for hh in range(2):
                qm = head_select(q, hh)
                s_lat = _dot_nt(qm, kwin)
                blocks = []
                for j in range(win // LANES):
                    b_lo = tb_ref[hh, 2 * j - oi + NA_KH - 1]
                    b_hi = tb_ref[hh, 2 * j + 1 - oi + NA_KH - 1]
                    blocks.append(s_lat[:, j * LANES:(j + 1) * LANES] + jnp.where(lane_q < 64, b_lo, b_hi))
                s_lat = jnp.concatenate(blocks, axis=1)
                outs.append(softmax_pv([s_lat, _dot_nt(qm, kz)], [vwin, vz]))
            o_ref[rr * w:(rr + 1) * w, :] = jnp.where(lane_q < 64, outs[0], outs[1]).astype(BF16)

    @pl.when(t >= n_lat_tiles)
    def _():
        q = q_ref[...]
        kz = kz_ref[...]
        vz = vz_ref[...]
        outs = [softmax_pv([_dot_nt(head_select(q, hh), kz)], [vz]) for hh in range(2)]
        lane = lax.broadcasted_iota(jnp.int32, outs[0].shape, 1)
        o_ref[...] = jnp.where(lane < 64, outs[0], outs[1]).astype(BF16)


def _na_call(q, k, v, table, n_batch, lat_len, ctx_len):
    n, d = q.shape
    tq = ctx_len
    n_lat_tiles = lat_len // tq
    ctx_blk0 = n_batch * lat_len // ctx_len

    def q_map(b, h, t):
        return (jnp.where(t < n_lat_tiles, b * n_lat_tiles + t, ctx_blk0 + b), h)

    return pl.pallas_call(
        functools.partial(_na_kernel, n_lat_tiles=n_lat_tiles, n_rows=lat_len // GRID_W),
        grid=(n_batch, NA_HEADS // 2, n_lat_tiles + 1),
        in_specs=[
            pl.BlockSpec((2,) + table.shape[1:], lambda b, h, t: (h, 0, 0, 0)),
            pl.BlockSpec((tq, LANES), q_map),
            pl.BlockSpec((lat_len, LANES), lambda b, h, t: (b, h)),
            pl.BlockSpec((ctx_len, LANES), lambda b, h, t: (ctx_blk0 + b, h)),
            pl.BlockSpec((lat_len, LANES), lambda b, h, t: (b, h)),
            pl.BlockSpec((ctx_len, LANES), lambda b, h, t: (ctx_blk0 + b, h)),
        ],
        out_specs=pl.BlockSpec((tq, LANES), q_map),
        out_shape=jax.ShapeDtypeStruct((n, d), BF16),
        compiler_params=_cparams(3),
        name="nbr_attn",
    )(table, q, k, k, v, v)


def _rope_tables(lat_len, head_dim, tm):
    f = head_dim // 4
    t = jnp.arange(lat_len)
    inv = ROPE_BASE ** (-jnp.arange(f, dtype=F32) / f)
    ang = jnp.concatenate([(t // GRID_W).astype(F32)[:, None] * inv, (t % GRID_W).astype(F32)[:, None] * inv], axis=-1)
    cos, sin = jnp.cos(ang), jnp.sin(ang)
    if head_dim // 2 < LANES:
        reps = LANES // head_dim
        cos = jnp.tile(jnp.concatenate([cos, cos], axis=-1), (1, reps))
        sin = jnp.tile(jnp.concatenate([-sin, sin], axis=-1), (1, reps))
    cos = jnp.concatenate([cos, jnp.ones((tm, LANES), F32)], axis=0)
    sin = jnp.concatenate([sin, jnp.zeros((tm, LANES), F32)], axis=0)
    return cos, sin


def kernel(x, c, ctx, c_ctx, w_ada, b_ada, norm_g, ffn_in, ffn_out, final_g, da_w_qkv, da_w_o, da_lambda, da_subln,
           ret_w_in, ret_w_o, ret_decay, na_w_qkv, na_w_o, na_rpb):
    n_batch, lat_len, d = x.shape
    ctx_len = ctx.shape[1]
    n_lat = n_batch * lat_len
    tm = 256
    assert n_batch + 1 <= MOD_ROWS and lat_len % tm == 0 and (n_batch * ctx_len) % tm == 0

    cond = jnp.concatenate([c, c_ctx[None, :], jnp.zeros((MOD_ROWS - n_batch - 1, d), F32)], axis=0)
    mods = _ada_call(cond, w_ada, b_ada).reshape(DEPTH, MOD_ROWS, 6, d)
    norm_g2 = norm_g.reshape(2 * DEPTH, 1, d)
    stream = jnp.concatenate([x.reshape(n_lat, d), ctx.reshape(n_batch * ctx_len, d)], axis=0)

    rope64 = _rope_tables(lat_len, d // (2 * DA_HEADS), tm)
    rope256 = _rope_tables(lat_len, d // RET_HEADS, tm)

    ia = ib = ic = 0
    for li in range(DEPTH):
        kind = li % 3
        final = li == DEPTH - 1
        if kind == 0:
            q, k, v = _pre_call(_pre_da_kernel, stream, mods, norm_g2, li, da_w_qkv[ia].astype(BF16), rope64,
                                (d, d, d), n_lat, lat_len, tm, "pre_diff_attn")
            lam_init = 0.8 - 0.6 * math.exp(-0.3 * li)
            a = _da_call(q, k, v, da_lambda[ia], da_subln[ia], lam_init, n_batch, lat_len, ctx_len, not final)
            w_o = da_w_o[ia]
            ia += 1
        elif kind == 1:
            q, k, v, sg = _pre_call(_pre_ret_kernel, stream, mods, norm_g2, li, ret_w_in[ib].astype(BF16), rope256,
                                    (d, d, 2 * d, 2 * d), n_lat, lat_len, tm, "pre_retention")
            a = _ret_call(q, k, v, sg, ret_decay[ib], n_batch, lat_len, ctx_len)
            w_o = ret_w_o[ib]
            ib += 1
        else:
            q, k, v = _pre_call(_pre_na_kernel, stream, mods, norm_g2, li, na_w_qkv[ic].astype(BF16), None,
                                (d, d, d), n_lat, lat_len, tm, "pre_nbr_attn")
            a = _na_call(q, k, v, _na_bias_table(na_rpb[ic]), n_batch, lat_len, ctx_len)
            w_o = na_w_o[ic]
            ic += 1
        stream = _post_call(a, stream, mods, norm_g2, li, w_o.astype(BF16), ffn_in[li].astype(BF16),
                            ffn_out[li].astype(BF16), final_g, n_lat, lat_len, tm, final)
    return stream.reshape(n_batch, lat_len, d)
```

```python
import functools
import math

import jax
import jax.numpy as jnp
from jax import lax
from jax.experimental import pallas as pl
from jax.experimental.pallas import tpu as pltpu

F32 = jnp.float32
BF16 = jnp.bfloat16

DEPTH = 4
GRID_W = 64
DA_HEADS = 8
RET_HEADS = 4
NA_HEADS = 16
NA_KH = 8
NA_KW = 16
ROPE_BASE = 10000.0
EPS = 1e-6
LOG2E = 1.4426950408889634
QK_SCALE = 0.125 * LOG2E
LANES = 128
RET_CHUNK = 256
MOD_ROWS = 16
VMEM_LIMIT = 56 * 1024 * 1024


def _cparams(n_axes):
    return pltpu.CompilerParams(dimension_semantics=("parallel",) * n_axes, vmem_limit_bytes=VMEM_LIMIT)


def _resident(shape):
    nd = len(shape)
    return pl.BlockSpec(shape, lambda *_: (0,) * nd, pipeline_mode=pl.Buffered(1))


def _rms(x):
    return x * lax.rsqrt(jnp.mean(x * x, axis=-1, keepdims=True) + EPS)


def _modulate(x, g, shift, scale):
    return _rms(x) * g * (1.0 + scale) + shift


def _silu(x):
    return x * jax.nn.sigmoid(x)


def _dot(a, b):
    return jnp.dot(a, b, preferred_element_type=F32)


def _dot_nt(a, b):
    return lax.dot_general(a, b, (((1,), (1,)), ((), ())), preferred_element_type=F32)


def _dot_tn(a, b):
    return lax.dot_general(a, b, (((0,), (0,)), ((), ())), preferred_element_type=F32)


def _ada_kernel(c_ref, w_ref, b_ref, o_ref):
    a = _silu(c_ref[...]).astype(BF16)
    o_ref[...] = _dot(a, w_ref[...].astype(BF16)) + b_ref[...]


def _ada_call(cond, w_ada, b_ada):
    depth, d, d6 = w_ada.shape
    tn = 1536
    return pl.pallas_call(
        _ada_kernel,
        grid=(depth, d6 // tn),
        in_specs=[
            pl.BlockSpec((MOD_ROWS, d), lambda l, j: (0, 0)),
            pl.BlockSpec((None, d, tn), lambda l, j: (l, 0, j)),
            pl.BlockSpec((None, 1, tn), lambda l, j: (l, 0, j)),
        ],
        out_specs=pl.BlockSpec((None, MOD_ROWS, tn), lambda l, j: (l, 0, j)),
        out_shape=jax.ShapeDtypeStruct((depth, MOD_ROWS, d6), F32),
        compiler_params=_cparams(2),
        name="ada_mod",
    )(cond, w_ada, b_ada.reshape(depth, 1, d6))


def _rope64_store(y, cos, sin, first_half, mul, dst):
    for j in range(y.shape[1] // LANES):
        yj = y[:, j * LANES:(j + 1) * LANES]
        sw = jnp.where(first_half, pltpu.roll(yj, LANES - 32, 1), pltpu.roll(yj, 32, 1))
        dst[:, j * LANES:(j + 1) * LANES] = ((yj * cos + sw * sin) * mul).astype(BF16)


def _pre_da_kernel(s_ref, mod_ref, g_ref, w_ref, cos_ref, sin_ref, q_ref, k_ref, v_ref):
    d = s_ref.shape[1]
    h = _modulate(s_ref[...], g_ref[...], mod_ref[0:1, :], mod_ref[1:2, :]).astype(BF16)
    cos = cos_ref[...]
    sin = sin_ref[...]
    first_half = (lax.broadcasted_iota(jnp.int32, cos.shape, 1) % 64) < 32
    _rope64_store(_dot(h, w_ref[:, 0:d]), cos, sin, first_half, QK_SCALE, q_ref)
    _rope64_store(_dot(h, w_ref[:, d:2 * d]), cos, sin, first_half, 1.0, k_ref)
    v_ref[...] = _dot(h, w_ref[:, 2 * d:3 * d]).astype(BF16)


def _pre_na_kernel(s_ref, mod_ref, g_ref, w_ref, q_ref, k_ref, v_ref):
    d = s_ref.shape[1]
    h = _modulate(s_ref[...], g_ref[...], mod_ref[0:1, :], mod_ref[1:2, :]).astype(BF16)
    q_ref[...] = (_dot(h, w_ref[:, 0:d]) * QK_SCALE).astype(BF16)
    k_ref[...] = _dot(h, w_ref[:, d:2 * d]).astype(BF16)
    v_ref[...] = _dot(h, w_ref[:, 2 * d:3 * d]).astype(BF16)


def _pre_ret_kernel(s_ref, mod_ref, g_ref, w_ref, cos_ref, sin_ref, q_ref, k_ref, v_ref, sg_ref):
    d = s_ref.shape[1]
    h = _modulate(s_ref[...], g_ref[...], mod_ref[0:1, :], mod_ref[1:2, :]).astype(BF16)
    cos = cos_ref[...]
    sin = sin_ref[...]
    dk = d // RET_HEADS
    for dst, col0, mul in ((q_ref, 0, 1.0), (k_ref, d, dk ** -0.5)):
        y = _dot(h, w_ref[:, col0:col0 + d])
        for hh in range(RET_HEADS):
            a0 = hh * dk
            x1 = y[:, a0:a0 + LANES]
            x2 = y[:, a0 + LANES:a0 + 2 * LANES]
            dst[:, a0:a0 + LANES] = ((x1 * cos - x2 * sin) * mul).astype(BF16)
            dst[:, a0 + LANES:a0 + 2 * LANES] = ((x2 * cos + x1 * sin) * mul).astype(BF16)
    v_ref[...] = _dot(h, w_ref[:, 2 * d:4 * d]).astype(BF16)
    sg_ref[...] = _silu(_dot(h, w_ref[:, 4 * d:6 * d])).astype(BF16)


def _pre_call(kernel_fn, stream, mods, norm_g2, li, w, rope, out_widths, n_lat, lat_len, tm, name):
    n, d = stream.shape
    n_lat_tiles = n_lat // tm
    tiles_per_batch = lat_len // tm
    n_batch = n_lat // lat_len

    def mod_map(i):
        return (li, jnp.where(i < n_lat_tiles, i // tiles_per_batch, n_batch), 0, 0)

    def rope_map(i):
        return (jnp.where(i < n_lat_tiles, i % tiles_per_batch, tiles_per_batch), 0)

    in_specs = [
        pl.BlockSpec((tm, d), lambda i: (i, 0)),
        pl.BlockSpec((None, None, 6, d), mod_map),
        pl.BlockSpec((None, 1, d), lambda i: (2 * li, 0, 0)),
        _resident(w.shape),
    ]
    args = [stream, mods, norm_g2, w]
    if rope is not None:
        in_specs += [pl.BlockSpec((tm, LANES), rope_map)] * 2
        args += list(rope)
    return pl.pallas_call(
        kernel_fn,
        grid=(n // tm,),
        in_specs=in_specs,
        out_specs=[pl.BlockSpec((tm, wd), lambda i: (i, 0)) for wd in out_widths],
        out_shape=[jax.ShapeDtypeStruct((n, wd), BF16) for wd in out_widths],
        compiler_params=_cparams(1),
        name=name,
    )(*args)


def _post_kernel(a_ref, s_ref, mod_ref, g_ref, wo_ref, win_ref, wout_ref, *rest, final):
    o_ref = rest[-1]
    fh = wout_ref.shape[0]
    x = s_ref[...] + mod_ref[2:3, :] * _dot(a_ref[...], wo_ref[...])
    h = _modulate(x, g_ref[...], mod_ref[3:4, :], mod_ref[4:5, :]).astype(BF16)
    hm = _dot(h, win_ref[...])
    act = (_silu(hm[:, :fh]) * hm[:, fh:]).astype(BF16)
    x = x + mod_ref[5:6, :] * _dot(act, wout_ref[...])
    if final:
        o_ref[...] = _rms(x) * rest[0][...]
    else:
        o_ref[...] = x


def _post_call(a, stream, mods, norm_g2, li, w_o, w_in, w_out, final_g, n_lat, lat_len, tm, final):
    n, d = stream.shape
    n_rows = n_lat if final else n
    n_lat_tiles = n_lat // tm
    tiles_per_batch = lat_len // tm
    n_batch = n_lat // lat_len

    def mod_map(i):
        return (li, jnp.where(i < n_lat_tiles, i // tiles_per_batch, n_batch), 0, 0)

    in_specs = [
        pl.BlockSpec((tm, a.shape[1]), lambda i: (i, 0)),
        pl.BlockSpec((tm, d), lambda i: (i, 0)),
        pl.BlockSpec((None, None, 6, d), mod_map),
        pl.BlockSpec((None, 1, d), lambda i: (2 * li + 1, 0, 0)),
        _resident(w_o.shape),
        _resident(w_in.shape),
        _resident(w_out.shape),
    ]
    args = [a, stream, mods, norm_g2, w_o, w_in, w_out]
    if final:
        in_specs.append(pl.BlockSpec((1, d), lambda i: (0, 0)))
        args.append(final_g.reshape(1, d))
    return pl.pallas_call(
        functools.partial(_post_kernel, final=final),
        grid=(n_rows // tm,),
        in_specs=in_specs,
        out_specs=pl.BlockSpec((tm, d), lambda i: (i, 0)),
        out_shape=jax.ShapeDtypeStruct((n_rows, d), F32),
        compiler_params=_cparams(1),
        name="post_final" if final else "post",
    )(*args)


def _with_ones_column(v):
    extra = (lax.broadcasted_iota(jnp.int32, v.shape, 1) == 0).astype(F32).astype(BF16)
    return jnp.concatenate([v, extra], axis=1)


def _softmax_pv(scores, values):
    m = scores[0].max(axis=-1, keepdims=True)
    for s in scores[1:]:
        m = jnp.maximum(m, s.max(axis=-1, keepdims=True))
    r = None
    for s, v in zip(scores, values):
        part = _dot(jnp.exp2(s - m).astype(BF16), v)
        r = part if r is None else r + part
    return r[:, :LANES] / r[:, LANES:LANES + 1]


def _da_kernel(lam_ref, sub_ref, q_ref, kx_ref, kz_ref, vx_ref, vz_ref, o_ref, *, lam_init, n_lat_tiles):
    t = pl.program_id(2)
    lp = lam_ref[...]
    lam = (jnp.exp(jnp.sum(lp[0:1, :] * lp[1:2, :], axis=-1, keepdims=True))
           - jnp.exp(jnp.sum(lp[2:3, :] * lp[3:4, :], axis=-1, keepdims=True)) + lam_init)
    q = q_ref[...].astype(F32)
    lane = lax.broadcasted_iota(jnp.int32, q.shape, 1)
    q_maps = (jnp.where(lane < 64, q, 0.0).astype(BF16), jnp.where(lane >= 64, q, 0.0).astype(BF16))

    def attend(keys, values):
        values = [_with_ones_column(v) for v in values]
        maps = [_softmax_pv([_dot_nt(qm, k) for k in keys], values) for qm in q_maps]
        acc = maps[0] - lam * maps[1]
        o_ref[...] = (_rms(acc) * sub_ref[...] * (1.0 - lam_init)).astype(BF16)

    @pl.when(t < n_lat_tiles)
    def _():
        attend((kx_ref[...], kz_ref[...]), (vx_ref[...], vz_ref[...]))

    @pl.when(t >= n_lat_tiles)
    def _():
        attend((kz_ref[...],), (vz_ref[...],))


def _da_call(q, k, v, lam_p, subln, lam_init, n_batch, lat_len, ctx_len, need_ctx):
    n, d = q.shape
    tq = ctx_len
    n_lat_tiles = lat_len // tq
    ctx_blk0 = n_batch * lat_len // ctx_len

    def q_map(b, h, t):
        return (jnp.where(t < n_lat_tiles, b * n_lat_tiles + t, ctx_blk0 + b), h)

    return pl.pallas_call(
        functools.partial(_da_kernel, lam_init=lam_init, n_lat_tiles=n_lat_tiles),
        grid=(n_batch, DA_HEADS, n_lat_tiles + (1 if need_ctx else 0)),
        in_specs=[
            pl.BlockSpec(lam_p.shape, lambda b, h, t: (0, 0)),
            pl.BlockSpec((1, LANES), lambda b, h, t: (0, 0)),
            pl.BlockSpec((tq, LANES), q_map),
            pl.BlockSpec((lat_len, LANES), lambda b, h, t: (b, h)),
            pl.BlockSpec((ctx_len, LANES), lambda b, h, t: (ctx_blk0 + b, h)),
            pl.BlockSpec((lat_len, LANES), lambda b, h, t: (b, h)),
            pl.BlockSpec((ctx_len, LANES), lambda b, h, t: (ctx_blk0 + b, h)),
        ],
        out_specs=pl.BlockSpec((tq, LANES), q_map),
        out_shape=jax.ShapeDtypeStruct((n if need_ctx else n_batch * lat_len, d), BF16),
        compiler_params=_cparams(3),
        name="diff_attn",
    )(lam_p, subln.reshape(1, LANES), q, k, k, v, v)


def _ret_kernel(dec_ref, qx_ref, qz_ref, kx_ref, kz_ref, vx_ref, vz_ref, gx_ref, gz_ref, ox_ref, oz_ref,
                acc_ref, dsb_ref):
    c = RET_CHUNK
    n_chunks = qx_ref.shape[0] // c
    head = pl.program_id(1)
    lgs = jax.nn.log_sigmoid(dec_ref[...])
    sel = lax.broadcasted_iota(jnp.int32, lgs.shape, 1) == head
    lgs = jnp.sum(jnp.where(sel, lgs, 0.0), axis=-1, keepdims=True)
    lg_f = lgs[0:1, :]
    lg_b = lgs[1:2, :]

    ii = lax.broadcasted_iota(jnp.int32, (c, c), 0)
    jj = lax.broadcasted_iota(jnp.int32, (c, c), 1)
    diff = (ii - jj).astype(F32)
    dmat = jnp.where(diff >= 0.0, jnp.exp(jnp.maximum(diff, 0.0) * lg_f), jnp.exp(jnp.maximum(-diff, 0.0) * lg_b))
    idx = lax.broadcasted_iota(jnp.int32, (c, 1), 0).astype(F32)
    xi_f = jnp.exp((idx + 1.0) * lg_f)
    xi_b = jnp.exp((c - idx) * lg_b)
    zeta_f = jnp.exp((c - 1.0 - idx) * lg_f)
    zeta_b = jnp.exp(idx * lg_b)
    decay_f = jnp.exp(c * lg_f)
    decay_b = jnp.exp(c * lg_b)

    def intra(q, k, v):
        return _dot((_dot_nt(q, k) * dmat).astype(BF16), v)

    def state_updates(k, v):
        k32 = k.astype(F32)
        return _dot_tn((k32 * zeta_f).astype(BF16), v), _dot_tn((k32 * zeta_b).astype(BF16), v)

    def finish(o, gate):
        return (_rms(o) * gate.astype(F32)).astype(BF16)

    qz, kz, vz = qz_ref[...], kz_ref[...], vz_ref[...]
    oz_ref[...] = finish(intra(qz, kz, vz), gz_ref[...])
    s_f, s_b = state_updates(kz, vz)

    for i in range(n_chunks):
        rows = slice(i * c, (i + 1) * c)
        q, k, v = qx_ref[rows, :], kx_ref[rows, :], vx_ref[rows, :]
        acc_ref[rows, :] = intra(q, k, v) + xi_f * _dot(q, s_f.astype(BF16))
        d_f, d_b = state_updates(k, v)
        dsb_ref[i] = d_b
        s_f = decay_f * s_f + d_f

    for i in reversed(range(n_chunks)):
        rows = slice(i * c, (i + 1) * c)
        o = acc_ref[rows, :] + xi_b * _dot(qx_ref[rows, :], s_b.astype(BF16))
        ox_ref[rows, :] = finish(o, gx_ref[rows, :])
        s_b = decay_b * s_b + dsb_ref[i]


def _ret_call(q, k, v, sg, decay, n_batch, lat_len, ctx_len):
    n, dq = q.shape
    dv = v.shape[1]
    dkh = dq // RET_HEADS
    dvh = dv // RET_HEADS
    ctx_blk0 = n_batch * lat_len // ctx_len

    def lat(b, h):
        return (b, h)

    def ctx(b, h):
        return (ctx_blk0 + b, h)

    ox, oz = pl.pallas_call(
        _ret_kernel,
        grid=(n_batch, RET_HEADS),
        in_specs=[
            pl.BlockSpec(decay.shape, lambda b, h: (0, 0)),
            pl.BlockSpec((lat_len, dkh), lat), pl.BlockSpec((ctx_len, dkh), ctx),
            pl.BlockSpec((lat_len, dkh), lat), pl.BlockSpec((ctx_len, dkh), ctx),
            pl.BlockSpec((lat_len, dvh), lat), pl.BlockSpec((ctx_len, dvh), ctx),
            pl.BlockSpec((lat_len, dvh), lat), pl.BlockSpec((ctx_len, dvh), ctx),
        ],
        out_specs=[pl.BlockSpec((lat_len, dvh), lat), pl.BlockSpec((ctx_len, dvh), lambda b, h: (b, h))],
        out_shape=[jax.ShapeDtypeStruct((n_batch * lat_len, dv), BF16),
                   jax.ShapeDtypeStruct((n_batch * ctx_len, dv), BF16)],
        scratch_shapes=[pltpu.VMEM((lat_len, dvh), F32),
                        pltpu.VMEM((lat_len // RET_CHUNK, dkh, dvh), F32)],
        compiler_params=_cparams(2),
        name="retention",
    )(decay, q, q, k, k, v, v, sg, sg)
    return jnp.concatenate([ox, oz], axis=0)


def _na_tile_plan(n_rows, rows_per_tile):
    kh = min(NA_KH, n_rows)
    span = kh + rows_per_tile
    classes, tile_class = [], []
    for t in range(n_rows // rows_per_tile):
        r0 = t * rows_per_tile
        start = min(max(r0 - kh // 2, 0), n_rows - span)
        sig = []
        for r in range(r0, r0 + rows_per_tile):
            rs = min(max(r - kh // 2, 0), n_rows - kh)
            assert start <= rs and rs + kh <= start + span
            sig.append((rs - start, start - r + NA_KH - 1))
        sig = tuple(sig)
        if sig not in classes:
            classes.append(sig)
        tile_class.append(classes.index(sig))
    return tile_class, classes


def _na_kernel(rpb_ref, q_ref, kx_ref, kz_ref, vx_ref, vz_ref, o_ref, tdup_ref, bias_ref, *, n_lat_tiles, n_rows):
    pair = pl.program_id(0)
    b = pl.program_id(1)
    t = pl.program_id(2)
    w = GRID_W
    rows_per_tile = q_ref.shape[0] // w
    kh = min(NA_KH, n_rows)
    span = kh + rows_per_tile
    tile_class, classes = _na_tile_plan(n_rows, rows_per_tile)
    n_dr = 2 * NA_KH - 1
    n_dc = 2 * NA_KW - 1
    neg_inf = jnp.full((w, LANES), -jnp.inf, F32)

    @pl.when(jnp.logical_and(b == 0, t == 0))
    def _():
        col_q = lax.broadcasted_iota(jnp.int32, (w, LANES), 0)
        lane = lax.broadcasted_iota(jnp.int32, (w, LANES), 1)
        col_k = lane % w
        dc = col_k - col_q + (NA_KW - 1)
        cs = jnp.clip(col_q - NA_KW // 2, 0, w - NA_KW)
        in_cols = jnp.logical_and(col_k >= cs, col_k < cs + NA_KW)
        for hh in range(2):
            for dr in range(n_dr):
                base = ((pair * 2 + hh) * n_dr + dr) * n_dc
                toe = lax.fori_loop(0, n_dc, lambda i, acc: jnp.where(dc == i, rpb_ref[base + i], acc), neg_inf)
                tdup_ref[hh, dr] = jnp.where(in_cols, toe * LOG2E, -jnp.inf)
        for hh in range(2):
            for ci, sig in enumerate(classes):
                for rr, (first, dr0) in enumerate(sig):
                    for j in range(span * w // LANES):
                        halves = []
                        for a in (2 * j, 2 * j + 1):
                            ok = first <= a < first + kh
                            halves.append(tdup_ref[hh, dr0 + a] if ok else neg_inf)
                        bias_ref[hh, ci, rr * w:(rr + 1) * w, j * LANES:(j + 1) * LANES] = (
                            jnp.where(lane < w, halves[0], halves[1]))

    def head_select(x, hh):
        lane = lax.broadcasted_iota(jnp.int32, x.shape, 1)
        keep = (lane < 64) if hh == 0 else (lane >= 64)
        return jnp.where(keep, x.astype(F32), 0.0).astype(BF16)

    def merge_heads(outs):
        lane = lax.broadcasted_iota(jnp.int32, outs[0].shape, 1)
        return jnp.where(lane < 64, outs[0], outs[1]).astype(BF16)

    @pl.when(t < n_lat_tiles)
    def _():
        start_row = jnp.clip(t * rows_per_tile - kh // 2, 0, n_rows - span)
        start = pl.multiple_of(start_row * w, w)
        kwin = kx_ref[pl.ds(start, span * w), :]
        vwin = _with_ones_column(vx_ref[pl.ds(start, span * w), :])
        kz = kz_ref[...]
        vz = _with_ones_column(vz_ref[...])
        cls = jnp.int32(tile_class[0])
        for tt in range(1, n_lat_tiles):
            cls = jnp.where(t == tt, tile_class[tt], cls)
        q = q_ref[...]
        outs = []
        for hh in range(2):
            qm = head_select(q, hh)
            outs.append(_softmax_pv([_dot_nt(qm, kwin) + bias_ref[hh, cls], _dot_nt(qm, kz)], [vwin, vz]))
        o_ref[...] = merge_heads(outs)

    @pl.when(t >= n_lat_tiles)
    def _():
        q = q_ref[...]
        kz = kz_ref[...]
        vz = _with_ones_column(vz_ref[...])
        o_ref[...] = merge_heads([_softmax_pv([_dot_nt(head_select(q, hh), kz)], [vz]) for hh in range(2)])


def _na_call(q, k, v, rpb, n_batch, lat_len, ctx_len):
    n, d = q.shape
    tq = ctx_len
    n_lat_tiles = lat_len // tq
    ctx_blk0 = n_batch * lat_len // ctx_len
    n_rows = lat_len // GRID_W
    rows_per_tile = tq // GRID_W
    span = min(NA_KH, n_rows) + rows_per_tile
    n_classes = len(_na_tile_plan(n_rows, rows_per_tile)[1])

    def q_map(h, b, t):
        return (jnp.where(t < n_lat_tiles, b * n_lat_tiles + t, ctx_blk0 + b), h)

    return pl.pallas_call(
        functools.partial(_na_kernel, n_lat_tiles=n_lat_tiles, n_rows=n_rows),
        grid=(NA_HEADS // 2, n_batch, n_lat_tiles + 1),
        in_specs=[
            pl.BlockSpec(memory_space=pltpu.SMEM),
            pl.BlockSpec((tq, LANES), q_map),
            pl.BlockSpec((lat_len, LANES), lambda h, b, t: (b, h)),
            pl.BlockSpec((ctx_len, LANES), lambda h, b, t: (ctx_blk0 + b, h)),
            pl.BlockSpec((lat_len, LANES), lambda h, b, t: (b, h)),
            pl.BlockSpec((ctx_len, LANES), lambda h, b, t: (ctx_blk0 + b, h)),
        ],
        out_specs=pl.BlockSpec((tq, LANES), q_map),
        out_shape=jax.ShapeDtypeStruct((n, d), BF16),
        scratch_shapes=[pltpu.VMEM((2, 2 * NA_KH - 1, GRID_W, LANES), F32),
                        pltpu.VMEM((2, n_classes, tq, span * GRID_W), F32)],
        compiler_params=pltpu.CompilerParams(dimension_semantics=("arbitrary",) * 3, vmem_limit_bytes=VMEM_LIMIT),
        name="nbr_attn",
    )(rpb.astype(F32).reshape(-1), q, k, k, v, v)


def _rope_tables(lat_len, head_dim, tm):
    f = head_dim // 4
    t = jnp.arange(lat_len)
    inv = ROPE_BASE ** (-jnp.arange(f, dtype=F32) / f)
    ang = jnp.concatenate([(t // GRID_W).astype(F32)[:, None] * inv, (t % GRID_W).astype(F32)[:, None] * inv], axis=-1)
    cos, sin = jnp.cos(ang), jnp.sin(ang)
    if head_dim // 2 < LANES:
        reps = LANES // head_dim
        cos = jnp.tile(jnp.concatenate([cos, cos], axis=-1), (1, reps))
        sin = jnp.tile(jnp.concatenate([-sin, sin], axis=-1), (1, reps))
    cos = jnp.concatenate([cos, jnp.ones((tm, LANES), F32)], axis=0)
    sin = jnp.concatenate([sin, jnp.zeros((tm, LANES), F32)], axis=0)
    return cos, sin


def kernel(x, c, ctx, c_ctx, w_ada, b_ada, norm_g, ffn_in, ffn_out, final_g, da_w_qkv, da_w_o, da_lambda, da_subln,
           ret_w_in, ret_w_o, ret_decay, na_w_qkv, na_w_o, na_rpb):
    n_batch, lat_len, d = x.shape
    ctx_len = ctx.shape[1]
    n_lat = n_batch * lat_len
    tm = 256
    assert n_batch + 1 <= MOD_ROWS and lat_len % tm == 0 and (n_batch * ctx_len) % tm == 0

    cond = jnp.concatenate([c, c_ctx[None, :], jnp.zeros((MOD_ROWS - n_batch - 1, d), F32)], axis=0)
    mods = _ada_call(cond, w_ada, b_ada).reshape(DEPTH, MOD_ROWS, 6, d)
    norm_g2 = norm_g.reshape(2 * DEPTH, 1, d)
    stream = jnp.concatenate([x.reshape(n_lat, d), ctx.reshape(n_batch * ctx_len, d)], axis=0)

    rope64 = _rope_tables(lat_len, d // (2 * DA_HEADS), tm)
    rope256 = _rope_tables(lat_len, d // RET_HEADS, tm)

    ia = ib = ic = 0
    for li in range(DEPTH):
        kind = li % 3
        final = li == DEPTH - 1
        if kind == 0:
            q, k, v = _pre_call(_pre_da_kernel, stream, mods, norm_g2, li, da_w_qkv[ia].astype(BF16), rope64,
                                (d, d, d), n_lat, lat_len, tm, "pre_diff_attn")
            lam_init = 0.8 - 0.6 * math.exp(-0.3 * li)
            a = _da_call(q, k, v, da_lambda[ia], da_subln[ia], lam_init, n_batch, lat_len, ctx_len, not final)
            w_o = da_w_o[ia]
            ia += 1
        elif kind == 1:
            q, k, v, sg = _pre_call(_pre_ret_kernel, stream, mods, norm_g2, li, ret_w_in[ib].astype(BF16), rope256,
                                    (d, d, 2 * d, 2 * d), n_lat, lat_len, tm, "pre_retention")
            a = _ret_call(q, k, v, sg, ret_decay[ib], n_batch, lat_len, ctx_len)
            w_o = ret_w_o[ib]
            ib += 1
        else:
            q, k, v = _pre_call(_pre_na_kernel, stream, mods, norm_g2, li, na_w_qkv[ic].astype(BF16), None,
                                (d, d, d), n_lat, lat_len, tm, "pre_nbr_attn")
            a = _na_call(q, k, v, na_rpb[ic], n_batch, lat_len, ctx_len)
            w_o = na_w_o[ic]
            ic += 1
        stream = _post_call(a, stream, mods, norm_g2, li, w_o.astype(BF16), ffn_in[li].astype(BF16),
                            ffn_out[li].astype(BF16), final_g, n_lat, lat_len, tm, final)
    return stream.reshape(n_batch, lat_len, d)
```

```python
import functools
import math

import jax
import jax.numpy as jnp
from jax import lax
from jax.experimental import pallas as pl
from jax.experimental.pallas import tpu as pltpu

F32 = jnp.float32
BF16 = jnp.bfloat16

DEPTH = 4
GRID_W = 64
DA_HEADS = 8
RET_HEADS = 4
NA_HEADS = 16
NA_KH = 8
NA_KW = 16
ROPE_BASE = 10000.0
EPS = 1e-6
LOG2E = 1.4426950408889634
QK_SCALE = 0.125 * LOG2E
LANES = 128
RET_CHUNK = 256
MOD_ROWS = 16
VMEM_LIMIT = 56 * 1024 * 1024


def _cparams(n_axes, ordered=False):
    sem = ("arbitrary" if ordered else "parallel",) * n_axes
    return pltpu.CompilerParams(dimension_semantics=sem, vmem_limit_bytes=VMEM_LIMIT)


def _resident(shape):
    nd = len(shape)
    return pl.BlockSpec(shape, lambda *_: (0,) * nd, pipeline_mode=pl.Buffered(1))


def _rms(x):
    return x * lax.rsqrt(jnp.mean(x * x, axis=-1, keepdims=True) + EPS)


def _modulate(x, g, shift, scale):
    return _rms(x) * g * (1.0 + scale) + shift


def _silu(x):
    return x * jax.nn.sigmoid(x)


def _dot(a, b):
    return jnp.dot(a, b, preferred_element_type=F32)


def _dot_nt(a, b):
    return lax.dot_general(a, b, (((1,), (1,)), ((), ())), preferred_element_type=F32)


def _dot_tn(a, b):
    return lax.dot_general(a, b, (((0,), (0,)), ((), ())), preferred_element_type=F32)


class _Rows:
    def __init__(self, n_lat, lat_len, n_ctx, tm):
        assert lat_len % tm == 0 and n_ctx % tm == 0
        self.n_lat, self.n_ctx, self.tm = n_lat, n_ctx, tm
        self.lat_tiles = n_lat // tm
        self.ctx_tiles = n_ctx // tm
        self.tiles_per_batch = lat_len // tm
        self.n_batch = n_lat // lat_len

    def is_lat(self, i):
        return i < self.lat_tiles

    def mod_row(self, i):
        return jnp.where(i < self.lat_tiles, i // self.tiles_per_batch, self.n_batch)

    def lat_block(self, i):
        return jnp.minimum(i, self.lat_tiles - 1)

    def ctx_block(self, i):
        return jnp.maximum(i - self.lat_tiles, 0)

    def split_specs(self, width):
        return [pl.BlockSpec((self.tm, width), lambda i: (self.lat_block(i), 0)),
                pl.BlockSpec((self.tm, width), lambda i: (self.ctx_block(i), 0))]


def _pick(i, rows, lat_ref, ctx_ref):
    return jnp.where(rows.is_lat(i), lat_ref[...], ctx_ref[...])


def _ada_kernel(c_ref, w_ref, b_ref, o_ref):
    a = _silu(c_ref[...]).astype(BF16)
    o_ref[...] = _dot(a, w_ref[...].astype(BF16)) + b_ref[...]


def _ada_call(cond, w_ada, b_ada):
    depth, d, d6 = w_ada.shape
    tn = 1536
    return pl.pallas_call(
        _ada_kernel,
        grid=(depth, d6 // tn),
        in_specs=[
            pl.BlockSpec((MOD_ROWS, d), lambda l, j: (0, 0)),
            pl.BlockSpec((None, d, tn), lambda l, j: (l, 0, j)),
            pl.BlockSpec((None, 1, tn), lambda l, j: (l, 0, j)),
        ],
        out_specs=pl.BlockSpec((None, MOD_ROWS, tn), lambda l, j: (l, 0, j)),
        out_shape=jax.ShapeDtypeStruct((depth, MOD_ROWS, d6), F32),
        compiler_params=_cparams(2),
        name="ada_mod",
    )(cond, w_ada, b_ada.reshape(depth, 1, d6))


def _rope64_store(y, cos, sin, first_half, mul, dst):
    for j in range(y.shape[1] // LANES):
        yj = y[:, j * LANES:(j + 1) * LANES]
        sw = jnp.where(first_half, pltpu.roll(yj, LANES - 32, 1), pltpu.roll(yj, 32, 1))
        dst[:, j * LANES:(j + 1) * LANES] = ((yj * cos + sw * sin) * mul).astype(BF16)


def _pre_hidden(rows, s_refs, mod_ref, g_ref):
    x = s_refs[0][...] if len(s_refs) == 1 else _pick(pl.program_id(0), rows, *s_refs)
    return _modulate(x, g_ref[...], mod_ref[0:1, :], mod_ref[1:2, :]).astype(BF16)


def _pre_da_kernel(*refs, rows, n_streams):
    s_refs, (mod_ref, g_ref, w_ref, cos_ref, sin_ref, q_ref, k_ref, v_ref) = refs[:n_streams], refs[n_streams:]
    d = w_ref.shape[0]
    h = _pre_hidden(rows, s_refs, mod_ref, g_ref)
    cos = cos_ref[...]
    sin = sin_ref[...]
    first_half = (lax.broadcasted_iota(jnp.int32, cos.shape, 1) % 64) < 32
    _rope64_store(_dot(h, w_ref[:, 0:d]), cos, sin, first_half, QK_SCALE, q_ref)
    _rope64_store(_dot(h, w_ref[:, d:2 * d]), cos, sin, first_half, 1.0, k_ref)
    v_ref[...] = _dot(h, w_ref[:, 2 * d:3 * d]).astype(BF16)


def _pre_na_kernel(*refs, rows, n_streams):
    s_refs, (mod_ref, g_ref, w_ref, q_ref, k_ref, v_ref) = refs[:n_streams], refs[n_streams:]
    d = w_ref.shape[0]
    h = _pre_hidden(rows, s_refs, mod_ref, g_ref)
    q_ref[...] = (_dot(h, w_ref[:, 0:d]) * QK_SCALE).astype(BF16)
    k_ref[...] = _dot(h, w_ref[:, d:2 * d]).astype(BF16)
    v_ref[...] = _dot(h, w_ref[:, 2 * d:3 * d]).astype(BF16)


def _pre_ret_kernel(*refs, rows, n_streams):
    s_refs, (mod_ref, g_ref, w_ref, cos_ref, sin_ref, q_ref, k_ref, v_ref, sg_ref) = refs[:n_streams], refs[n_streams:]
    d = w_ref.shape[0]
    h = _pre_hidden(rows, s_refs, mod_ref, g_ref)
    cos = cos_ref[...]
    sin = sin_ref[...]
    dk = d // RET_HEADS
    for dst, col0, mul in ((q_ref, 0, 1.0), (k_ref, d, dk ** -0.5)):
        y = _dot(h, w_ref[:, col0:col0 + d])
        for hh in range(RET_HEADS):
            a0 = hh * dk
            x1 = y[:, a0:a0 + LANES]
            x2 = y[:, a0 + LANES:a0 + 2 * LANES]
            dst[:, a0:a0 + LANES] = ((x1 * cos - x2 * sin) * mul).astype(BF16)
            dst[:, a0 + LANES:a0 + 2 * LANES] = ((x2 * cos + x1 * sin) * mul).astype(BF16)
    v_ref[...] = _dot(h, w_ref[:, 2 * d:4 * d]).astype(BF16)
    sg_ref[...] = _silu(_dot(h, w_ref[:, 4 * d:6 * d])).astype(BF16)


def _pre_call(kernel_fn, streams, mods, norm_g2, li, w, rope, out_widths, rows, name):
    d = w.shape[0]
    tm = rows.tm
    n = rows.n_lat + rows.n_ctx

    def rope_map(i):
        return (jnp.where(rows.is_lat(i), i % rows.tiles_per_batch, rows.tiles_per_batch), 0)

    in_specs = [pl.BlockSpec((tm, d), lambda i: (i, 0))] if len(streams) == 1 else rows.split_specs(d)
    in_specs += [
        pl.BlockSpec((None, None, 6, d), lambda i: (li, rows.mod_row(i), 0, 0)),
        pl.BlockSpec((None, 1, d), lambda i: (2 * li, 0, 0)),
        _resident(w.shape),
    ]
    args = list(streams) + [mods, norm_g2, w]
    if rope is not None:
        in_specs += [pl.BlockSpec((tm, LANES), rope_map)] * 2
        args += list(rope)
    return pl.pallas_call(
        functools.partial(kernel_fn, rows=rows, n_streams=len(streams)),
        grid=(n // tm,),
        in_specs=in_specs,
        out_specs=[pl.BlockSpec((tm, wd), lambda i: (i, 0)) for wd in out_widths],
        out_shape=[jax.ShapeDtypeStruct((n, wd), BF16) for wd in out_widths],
        compiler_params=_cparams(1),
        name=name,
    )(*args)


def _post_kernel(*refs, rows, n_a, n_streams, final):
    a_refs, s_refs = refs[:n_a], refs[n_a:n_a + n_streams]
    mod_ref, g_ref, wo_ref, win_ref, wout_ref = refs[n_a + n_streams:n_a + n_streams + 5]
    o_ref = refs[-1]
    i = pl.program_id(0)
    fh = wout_ref.shape[0]
    a = a_refs[0][...] if n_a == 1 else _pick(i, rows, *a_refs)
    s = s_refs[0][...] if n_streams == 1 else _pick(i, rows, *s_refs)
    x = s + mod_ref[2:3, :] * _dot(a, wo_ref[...])
    h = _modulate(x, g_ref[...], mod_ref[3:4, :], mod_ref[4:5, :]).astype(BF16)
    hm = _dot(h, win_ref[...])
    act = (_silu(hm[:, :fh]) * hm[:, fh:]).astype(BF16)
    x = x + mod_ref[5:6, :] * _dot(act, wout_ref[...])
    if final:
        o_ref[...] = _rms(x) * refs[-2][...]
    else:
        o_ref[...] = x


def _post_call(a_parts, streams, mods, norm_g2, li, w_o, w_in, w_out, final_g, rows, final):
    d = w_o.shape[1]
    tm = rows.tm
    n_rows = rows.n_lat if final else rows.n_lat + rows.n_ctx
    ka = a_parts[0].shape[1]
    in_specs = [pl.BlockSpec((tm, ka), lambda i: (i, 0))] if len(a_parts) == 1 else rows.split_specs(ka)
    in_specs += [pl.BlockSpec((tm, d), lambda i: (i, 0))] if len(streams) == 1 else rows.split_specs(d)
    in_specs += [
        pl.BlockSpec((None, None, 6, d), lambda i: (li, rows.mod_row(i), 0, 0)),
        pl.BlockSpec((None, 1, d), lambda i: (2 * li + 1, 0, 0)),
        _resident(w_o.shape),
        _resident(w_in.shape),
        _resident(w_out.shape),
    ]
    args = list(a_parts) + list(streams) + [mods, norm_g2, w_o, w_in, w_out]
    if final:
        in_specs.append(pl.BlockSpec((1, d), lambda i: (0, 0)))
        args.append(final_g.reshape(1, d))
    return pl.pallas_call(
        functools.partial(_post_kernel, rows=rows, n_a=len(a_parts), n_streams=len(streams), final=final),
        grid=(n_rows // tm,),
        in_specs=in_specs,
        out_specs=pl.BlockSpec((tm, d), lambda i: (i, 0)),
        out_shape=jax.ShapeDtypeStruct((n_rows, d), F32),
        compiler_params=_cparams(1),
        name="post_final" if final else "post",
    )(*args)


def _with_ones_column(v):
    extra = (lax.broadcasted_iota(jnp.int32, v.shape, 1) == 0).astype(F32).astype(BF16)
    return jnp.concatenate([v, extra], axis=1)


def _row_max(scores):
    m = scores[0].max(axis=-1, keepdims=True)
    for s in scores[1:]:
        m = jnp.maximum(m, s.max(axis=-1, keepdims=True))
    return m


def _normalised(r):
    return r[:, :LANES] / r[:, LANES:LANES + 1]


def _softmax_pv(scores, values):
    m = _row_max(scores)
    r = None
    for s, v in zip(scores, values):
        part = _dot(jnp.exp2(s - m).astype(BF16), v)
        r = part if r is None else r + part
    return _normalised(r)


def _lane_halves(q):
    q = q.astype(F32)
    lane = lax.broadcasted_iota(jnp.int32, q.shape, 1)
    return jnp.where(lane < 64, q, 0.0).astype(BF16), jnp.where(lane >= 64, q, 0.0).astype(BF16)


def _merge_halves(lo, hi):
    lane = lax.broadcasted_iota(jnp.int32, lo.shape, 1)
    return jnp.where(lane < 64, lo, hi)


def _da_kernel(*refs, lam_init, need_ctx):
    lam_ref, sub_ref, q0_ref, qa_ref, qb_ref = refs[:5]
    refs = refs[5:]
    if need_ctx:
        qz_ref, refs = refs[0], refs[1:]
    kx_ref, kz_ref, vx_ref, vz_ref, ox_ref = refs[:5]
    refs = refs[5:]
    if need_ctx:
        oz_ref, refs = refs[0], refs[1:]
    p0_ref, p1_ref, va_ref = refs

    u = pl.program_id(2)
    tq = qa_ref.shape[0]
    n_lat = kx_ref.shape[0]
    lp = lam_ref[...]
    lam = (jnp.exp(jnp.sum(lp[0:1, :] * lp[1:2, :], axis=-1, keepdims=True))
           - jnp.exp(jnp.sum(lp[2:3, :] * lp[3:4, :], axis=-1, keepdims=True)) + lam_init)

    def qk_softmax(q_ref, p_ref):
        for mi, qm in enumerate(_lane_halves(q_ref[...])):
            sx = _dot_nt(qm, kx_ref[...])
            sz = _dot_nt(qm, kz_ref[...])
            m = _row_max([sx, sz])
            p_ref[mi, :, :n_lat] = jnp.exp2(sx - m).astype(BF16)
            p_ref[mi, :, n_lat:] = jnp.exp2(sz - m).astype(BF16)

    def finish(maps):
        acc = maps[0] - lam * maps[1]
        return (_rms(acc) * sub_ref[...] * (1.0 - lam_init)).astype(BF16)

    def pv(p_ref):
        va = va_ref[...]
        return finish([_normalised(_dot(p_ref[mi], va)) for mi in range(2)])

    @pl.when(u == 0)
    def _():
        va_ref[:n_lat, :] = _with_ones_column(vx_ref[...])
        va_ref[n_lat:, :] = _with_ones_column(vz_ref[...])
        qk_softmax(q0_ref, p0_ref)
        if need_ctx:
            kz = kz_ref[...]
            vz = va_ref[n_lat:, :]
            oz_ref[...] = finish([_softmax_pv([_dot_nt(qm, kz)], [vz]) for qm in _lane_halves(qz_ref[...])])

    qk_softmax(qa_ref, p1_ref)
    ox_ref[:tq, :] = pv(p0_ref)
    qk_softmax(qb_ref, p0_ref)
    ox_ref[tq:, :] = pv(p1_ref)


def _attn_specs(n_batch, lat_len, ctx_len, tq, need_ctx):
    n_lat_tiles = lat_len // tq
    ctx_blk0 = n_batch * lat_len // ctx_len
    q_specs = [
        pl.BlockSpec((tq, LANES), lambda h, b, u: (b * n_lat_tiles, h)),
        pl.BlockSpec((tq, LANES), lambda h, b, u: (b * n_lat_tiles + 2 * u + 1, h)),
        pl.BlockSpec((tq, LANES), lambda h, b, u: (b * n_lat_tiles + jnp.minimum(2 * u + 2, n_lat_tiles - 1), h)),
    ]
    if need_ctx:
        q_specs.append(pl.BlockSpec((ctx_len, LANES), lambda h, b, u: (ctx_blk0 + b, h)))
    kv_specs = [
        pl.BlockSpec((lat_len, LANES), lambda h, b, u: (b, h)),
        pl.BlockSpec((ctx_len, LANES), lambda h, b, u: (ctx_blk0 + b, h)),
    ] * 2
    out_specs = [pl.BlockSpec((2 * tq, LANES), lambda h, b, u: (b * (n_lat_tiles // 2) + u, h))]
    if need_ctx:
        out_specs.append(pl.BlockSpec((ctx_len, LANES), lambda h, b, u: (b, h)))
    return q_specs, kv_specs, out_specs


def _da_call(q, k, v, lam_p, subln, lam_init, n_batch, lat_len, ctx_len, need_ctx):
    d = q.shape[1]
    tq = ctx_len
    assert lat_len % (2 * tq) == 0
    q_specs, kv_specs, out_specs = _attn_specs(n_batch, lat_len, ctx_len, tq, need_ctx)
    out_shape = [jax.ShapeDtypeStruct((n_batch * lat_len, d), BF16)]
    if need_ctx:
        out_shape.append(jax.ShapeDtypeStruct((n_batch * ctx_len, d), BF16))
    n_q = len(q_specs)
    return pl.pallas_call(
        functools.partial(_da_kernel, lam_init=lam_init, need_ctx=need_ctx),
        grid=(DA_HEADS, n_batch, lat_len // (2 * tq)),
        in_specs=[pl.BlockSpec(lam_p.shape, lambda h, b, u: (0, 0)),
                  pl.BlockSpec((1, LANES), lambda h, b, u: (0, 0))] + q_specs + kv_specs,
        out_specs=out_specs,
        out_shape=out_shape,
        scratch_shapes=[pltpu.VMEM((2, tq, lat_len + ctx_len), BF16),
                        pltpu.VMEM((2, tq, lat_len + ctx_len), BF16),
                        pltpu.VMEM((lat_len + ctx_len, 2 * LANES), BF16)],
        compiler_params=_cparams(3, ordered=True),
        name="diff_attn",
    )(lam_p, subln.reshape(1, LANES), *([q] * n_q), k, k, v, v)


def _ret_kernel(dec_ref, qx_ref, qz_ref, kx_ref, kz_ref, vx_ref, vz_ref, gx_ref, gz_ref, ox_ref, oz_ref,
                acc_ref, dsb_ref):
    c = RET_CHUNK
    n_chunks = qx_ref.shape[0] // c
    head = pl.program_id(1)
    lgs = jax.nn.log_sigmoid(dec_ref[...])
    sel = lax.broadcasted_iota(jnp.int32, lgs.shape, 1) == head
    lgs = jnp.sum(jnp.where(sel, lgs, 0.0), axis=-1, keepdims=True)
    lg_f = lgs[0:1, :]
    lg_b = lgs[1:2, :]

    ii = lax.broadcasted_iota(jnp.int32, (c, c), 0)
    jj = lax.broadcasted_iota(jnp.int32, (c, c), 1)
    diff = (ii - jj).astype(F32)
    dmat = jnp.where(diff >= 0.0, jnp.exp(jnp.maximum(diff, 0.0) * lg_f), jnp.exp(jnp.maximum(-diff, 0.0) * lg_b))
    idx = lax.broadcasted_iota(jnp.int32, (c, 1), 0).astype(F32)
    xi_f = jnp.exp((idx + 1.0) * lg_f)
    xi_b = jnp.exp((c - idx) * lg_b)
    zeta_f = jnp.exp((c - 1.0 - idx) * lg_f)
    zeta_b = jnp.exp(idx * lg_b)
    decay_f = jnp.exp(c * lg_f)
    decay_b = jnp.exp(c * lg_b)

    def intra(q, k, v):
        return _dot((_dot_nt(q, k) * dmat).astype(BF16), v)

    def state_updates(k, v):
        k32 = k.astype(F32)
        return _dot_tn((k32 * zeta_f).astype(BF16), v), _dot_tn((k32 * zeta_b).astype(BF16), v)

    def finish(o, gate):
        return (_rms(o) * gate.astype(F32)).astype(BF16)

    qz, kz, vz = qz_ref[...], kz_ref[...], vz_ref[...]
    oz_ref[...] = finish(intra(qz, kz, vz), gz_ref[...])
    s_f, s_b = state_updates(kz, vz)

    for i in range(n_chunks):
        rows = slice(i * c, (i + 1) * c)
        q, k, v = qx_ref[rows, :], kx_ref[rows, :], vx_ref[rows, :]
        acc_ref[rows, :] = intra(q, k, v) + xi_f * _dot(q, s_f.astype(BF16))
        d_f, d_b = state_updates(k, v)
        dsb_ref[i] = d_b
        s_f = decay_f * s_f + d_f

    for i in reversed(range(n_chunks)):
        rows = slice(i * c, (i + 1) * c)
        o = acc_ref[rows, :] + xi_b * _dot(qx_ref[rows, :], s_b.astype(BF16))
        ox_ref[rows, :] = finish(o, gx_ref[rows, :])
        s_b = decay_b * s_b + dsb_ref[i]


def _ret_call(q, k, v, sg, decay, n_batch, lat_len, ctx_len):
    dq = q.shape[1]
    dv = v.shape[1]
    dkh = dq // RET_HEADS
    dvh = dv // RET_HEADS
    ctx_blk0 = n_batch * lat_len // ctx_len

    def lat(b, h):
        return (b, h)

    def ctx(b, h):
        return (ctx_blk0 + b, h)

    return pl.pallas_call(
        _ret_kernel,
        grid=(n_batch, RET_HEADS),
        in_specs=[
            pl.BlockSpec(decay.shape, lambda b, h: (0, 0)),
            pl.BlockSpec((lat_len, dkh), lat), pl.BlockSpec((ctx_len, dkh), ctx),
            pl.BlockSpec((lat_len, dkh), lat), pl.BlockSpec((ctx_len, dkh), ctx),
            pl.BlockSpec((lat_len, dvh), lat), pl.BlockSpec((ctx_len, dvh), ctx),
            pl.BlockSpec((lat_len, dvh), lat), pl.BlockSpec((ctx_len, dvh), ctx),
        ],
        out_specs=[pl.BlockSpec((lat_len, dvh), lat), pl.BlockSpec((ctx_len, dvh), lambda b, h: (b, h))],
        out_shape=[jax.ShapeDtypeStruct((n_batch * lat_len, dv), BF16),
                   jax.ShapeDtypeStruct((n_batch * ctx_len, dv), BF16)],
        scratch_shapes=[pltpu.VMEM((lat_len, dvh), F32),
                        pltpu.VMEM((lat_len // RET_CHUNK, dkh, dvh), F32)],
        compiler_params=_cparams(2),
        name="retention",
    )(decay, q, q, k, k, v, v, sg, sg)


def _na_tile_plan(n_rows, rows_per_tile):
    kh = min(NA_KH, n_rows)
    span = kh + rows_per_tile
    classes, tile_class = [], []
    for t in range(n_rows // rows_per_tile):
        r0 = t * rows_per_tile
        start = min(max(r0 - kh // 2, 0), n_rows - span)
        sig = []
        for r in range(r0, r0 + rows_per_tile):
            rs = min(max(r - kh // 2, 0), n_rows - kh)
            assert start <= rs and rs + kh <= start + span
            sig.append((rs - start, start - r + NA_KH - 1))
        sig = tuple(sig)
        if sig not in classes:
            classes.append(sig)
        tile_class.append(classes.index(sig))
    return tile_class, classes


def _na_kernel(rpb_ref, q0_ref, qa_ref, qb_ref, qz_ref, kx_ref, kz_ref, vx_ref, vz_ref, ox_ref, oz_ref,
               tdup_ref, bias_ref, p0_ref, p1_ref, vxa_ref, vza_ref, *, n_rows):
    pair = pl.program_id(0)
    b = pl.program_id(1)
    u = pl.program_id(2)
    w = GRID_W
    tq = qa_ref.shape[0]
    rows_per_tile = tq // w
    n_lat_tiles = n_rows // rows_per_tile
    kh = min(NA_KH, n_rows)
    span = kh + rows_per_tile
    n_win = span * w
    tile_class, classes = _na_tile_plan(n_rows, rows_per_tile)
    n_dr = 2 * NA_KH - 1
    n_dc = 2 * NA_KW - 1
    neg_inf = jnp.full((w, LANES), -jnp.inf, F32)

    @pl.when(jnp.logical_and(b == 0, u == 0))
    def _():
        col_q = lax.broadcasted_iota(jnp.int32, (w, LANES), 0)
        lane = lax.broadcasted_iota(jnp.int32, (w, LANES), 1)
        col_k = lane % w
        dc = col_k - col_q + (NA_KW - 1)
        cs = jnp.clip(col_q - NA_KW // 2, 0, w - NA_KW)
        in_cols = jnp.logical_and(col_k >= cs, col_k < cs + NA_KW)
        for hh in range(2):
            for dr in range(n_dr):
                base = ((pair * 2 + hh) * n_dr + dr) * n_dc
                toe = lax.fori_loop(0, n_dc, lambda i, acc: jnp.where(dc == i, rpb_ref[base + i], acc), neg_inf)
                tdup_ref[hh, dr] = jnp.where(in_cols, toe * LOG2E, -jnp.inf)
        for hh in range(2):
            for ci, sig in enumerate(classes):
                for rr, (first, dr0) in enumerate(sig):
                    for j in range(n_win // LANES):
                        halves = []
                        for a in (2 * j, 2 * j + 1):
                            ok = first <= a < first + kh
                            halves.append(tdup_ref[hh, dr0 + a] if ok else neg_inf)
                        bias_ref[hh, ci, rr * w:(rr + 1) * w, j * LANES:(j + 1) * LANES] = (
                            jnp.where(lane < w, halves[0], halves[1]))

    def window_start(t):
        if isinstance(t, int):
            return min(max(t * rows_per_tile - kh // 2, 0), n_rows - span) * w
        start_row = jnp.clip(t * rows_per_tile - kh // 2, 0, n_rows - span)
        return pl.multiple_of(start_row * w, w)

    def tile_class_of(t):
        if isinstance(t, int):
            return tile_class[t]
        cls = jnp.int32(tile_class[0])
        for tt in range(1, n_lat_tiles):
            cls = jnp.where(t == tt, tile_class[tt], cls)
        return cls

    def qk_softmax(q_ref, t, p_ref):
        kwin = kx_ref[pl.ds(window_start(t), n_win), :]
        kz = kz_ref[...]
        cls = tile_class_of(t)
        for hh, qm in enumerate(_lane_halves(q_ref[...])):
            s_lat = _dot_nt(qm, kwin) + bias_ref[hh, cls]
            s_ctx = _dot_nt(qm, kz)
            m = _row_max([s_lat, s_ctx])
            p_ref[hh, :, :n_win] = jnp.exp2(s_lat - m).astype(BF16)
            p_ref[hh, :, n_win:] = jnp.exp2(s_ctx - m).astype(BF16)

    def pv(t, p_ref):
        vwin = vxa_ref[pl.ds(window_start(t), n_win), :]
        vz = vza_ref[...]
        outs = [_normalised(_dot(p_ref[hh, :, :n_win], vwin) + _dot(p_ref[hh, :, n_win:], vz)) for hh in range(2)]
        return _merge_halves(*outs).astype(BF16)

    @pl.when(u == 0)
    def _():
        vxa_ref[...] = _with_ones_column(vx_ref[...])
        vza_ref[...] = _with_ones_column(vz_ref[...])
        qk_softmax(q0_ref, 0, p0_ref)
        kz = kz_ref[...]
        vz = vza_ref[...]
        outs = [_softmax_pv([_dot_nt(qm, kz)], [vz]) for qm in _lane_halves(qz_ref[...])]
        oz_ref[...] = _merge_halves(*outs).astype(BF16)

    ox_ref[:tq, :] = pv(2 * u, p0_ref)
    qk_softmax(qa_ref, 2 * u + 1, p1_ref)
    ox_ref[tq:, :] = pv(2 * u + 1, p1_ref)
    qk_softmax(qb_ref, jnp.minimum(2 * u + 2, n_lat_tiles - 1), p0_ref)


def _na_call(q, k, v, rpb, n_batch, lat_len, ctx_len):
    d = q.shape[1]
    tq = ctx_len
    assert lat_len % (2 * tq) == 0 and tq % GRID_W == 0
    n_rows = lat_len // GRID_W
    rows_per_tile = tq // GRID_W
    n_win = (min(NA_KH, n_rows) + rows_per_tile) * GRID_W
    n_classes = len(_na_tile_plan(n_rows, rows_per_tile)[1])
    q_specs, kv_specs, out_specs = _attn_specs(n_batch, lat_len, ctx_len, tq, True)
    return pl.pallas_call(
        functools.partial(_na_kernel, n_rows=n_rows),
        grid=(NA_HEADS // 2, n_batch, lat_len // (2 * tq)),
        in_specs=[pl.BlockSpec(memory_space=pltpu.SMEM)] + q_specs + kv_specs,
        out_specs=out_specs,
        out_shape=[jax.ShapeDtypeStruct((n_batch * lat_len, d), BF16),
                   jax.ShapeDtypeStruct((n_batch * ctx_len, d), BF16)],
        scratch_shapes=[pltpu.VMEM((2, 2 * NA_KH - 1, GRID_W, LANES), F32),
                        pltpu.VMEM((2, n_classes, tq, n_win), F32),
                        pltpu.VMEM((2, tq, n_win + ctx_len), BF16),
                        pltpu.VMEM((2, tq, n_win + ctx_len), BF16),
                        pltpu.VMEM((lat_len, 2 * LANES), BF16),
                        pltpu.VMEM((ctx_len, 2 * LANES), BF16)],
        compiler_params=_cparams(3, ordered=True),
        name="nbr_attn",
    )(rpb.astype(F32).reshape(-1), q, q, q, q, k, k, v, v)


def _rope_tables(lat_len, head_dim, tm):
    f = head_dim // 4
    t = jnp.arange(lat_len)
    inv = ROPE_BASE ** (-jnp.arange(f, dtype=F32) / f)
    ang = jnp.concatenate([(t // GRID_W).astype(F32)[:, None] * inv, (t % GRID_W).astype(F32)[:, None] * inv], axis=-1)
    cos, sin = jnp.cos(ang), jnp.sin(ang)
    if head_dim // 2 < LANES:
        reps = LANES // head_dim
        cos = jnp.tile(jnp.concatenate([cos, cos], axis=-1), (1, reps))
        sin = jnp.tile(jnp.concatenate([-sin, sin], axis=-1), (1, reps))
    cos = jnp.concatenate([cos, jnp.ones((tm, LANES), F32)], axis=0)
    sin = jnp.concatenate([sin, jnp.zeros((tm, LANES), F32)], axis=0)
    return cos, sin


def kernel(x, c, ctx, c_ctx, w_ada, b_ada, norm_g, ffn_in, ffn_out, final_g, da_w_qkv, da_w_o, da_lambda, da_subln,
           ret_w_in, ret_w_o, ret_decay, na_w_qkv, na_w_o, na_rpb):
    n_batch, lat_len, d = x.shape
    ctx_len = ctx.shape[1]
    assert n_batch + 1 <= MOD_ROWS
    rows = _Rows(n_batch * lat_len, lat_len, n_batch * ctx_len, tm=256)

    cond = jnp.concatenate([c, c_ctx[None, :], jnp.zeros((MOD_ROWS - n_batch - 1, d), F32)], axis=0)
    mods = _ada_call(cond, w_ada, b_ada).reshape(DEPTH, MOD_ROWS, 6, d)
    norm_g2 = norm_g.reshape(2 * DEPTH, 1, d)
    streams = (x.reshape(rows.n_lat, d), ctx.reshape(rows.n_ctx, d))

    rope64 = _rope_tables(lat_len, d // (2 * DA_HEADS), rows.tm)
    rope256 = _rope_tables(lat_len, d // RET_HEADS, rows.tm)

    ia = ib = ic = 0
    for li in range(DEPTH):
        kind = li % 3
        final = li == DEPTH - 1
        if kind == 0:
            q, k, v = _pre_call(_pre_da_kernel, streams, mods, norm_g2, li, da_w_qkv[ia].astype(BF16), rope64,
                                (d, d, d), rows, "pre_diff_attn")
            lam_init = 0.8 - 0.6 * math.exp(-0.3 * li)
            a_parts = _da_call(q, k, v, da_lambda[ia], da_subln[ia], lam_init, n_batch, lat_len, ctx_len, not final)
            w_o = da_w_o[ia]
            ia += 1
        elif kind == 1:
            q, k, v, sg = _pre_call(_pre_ret_kernel, streams, mods, norm_g2, li, ret_w_in[ib].astype(BF16), rope256,
                                    (d, d, 2 * d, 2 * d), rows, "pre_retention")
            a_parts = _ret_call(q, k, v, sg, ret_decay[ib], n_batch, lat_len, ctx_len)
            w_o = ret_w_o[ib]
            ib += 1
        else:
            q, k, v = _pre_call(_pre_na_kernel, streams, mods, norm_g2, li, na_w_qkv[ic].astype(BF16), None,
                                (d, d, d), rows, "pre_nbr_attn")
            a_parts = _na_call(q, k, v, na_rpb[ic], n_batch, lat_len, ctx_len)
            w_o = na_w_o[ic]
            ic += 1
        streams = (_post_call(a_parts, streams, mods, norm_g2, li, w_o.astype(BF16), ffn_in[li].astype(BF16),
                              ffn_out[li].astype(BF16), final_g, rows, final),)
    return streams[0].reshape(n_batch, lat_len, d)
```

```python
import functools
import math

import jax
import jax.numpy as jnp
from jax import lax
from jax.experimental import pallas as pl
from jax.experimental.pallas import tpu as pltpu

F32 = jnp.float32
BF16 = jnp.bfloat16

DEPTH = 4
GRID_W = 64
DA_HEADS = 8
RET_HEADS = 4
NA_HEADS = 16
NA_KH = 8
NA_KW = 16
ROPE_BASE = 10000.0
EPS = 1e-6
LOG2E = 1.4426950408889634
QK_SCALE = 0.125 * LOG2E
LANES = 128
RET_CHUNK = 256
MOD_ROWS = 16
VMEM_LIMIT = 56 * 1024 * 1024


def _cparams(n_axes, ordered=False):
    sem = ("arbitrary" if ordered else "parallel",) * n_axes
    return pltpu.CompilerParams(dimension_semantics=sem, vmem_limit_bytes=VMEM_LIMIT)


def _resident(shape):
    nd = len(shape)
    return pl.BlockSpec(shape, lambda *_: (0,) * nd, pipeline_mode=pl.Buffered(1))


def _rms(x):
    return x * lax.rsqrt(jnp.mean(x * x, axis=-1, keepdims=True) + EPS)


def _modulate(x, g, shift, scale):
    return _rms(x) * g * (1.0 + scale) + shift


def _silu(x):
    return x * jax.nn.sigmoid(x)


def _dot(a, b):
    return jnp.dot(a, b, preferred_element_type=F32)


def _dot_nt(a, b):
    return lax.dot_general(a, b, (((1,), (1,)), ((), ())), preferred_element_type=F32)


def _dot_tn(a, b):
    return lax.dot_general(a, b, (((0,), (0,)), ((), ())), preferred_element_type=F32)


class _Rows:
    def __init__(self, n_lat, lat_len, n_ctx, tm):
        assert lat_len % tm == 0 and n_ctx % tm == 0
        self.n_lat, self.n_ctx, self.tm = n_lat, n_ctx, tm
        self.lat_tiles = n_lat // tm
        self.ctx_tiles = n_ctx // tm
        self.tiles_per_batch = lat_len // tm
        self.n_batch = n_lat // lat_len

    def is_lat(self, i):
        return i < self.lat_tiles

    def mod_row(self, i):
        return jnp.where(i < self.lat_tiles, i // self.tiles_per_batch, self.n_batch)

    def lat_block(self, i):
        return jnp.minimum(i, self.lat_tiles - 1)

    def ctx_block(self, i):
        return jnp.maximum(i - self.lat_tiles, 0)

    def split_specs(self, width):
        return [pl.BlockSpec((self.tm, width), lambda i: (self.lat_block(i), 0)),
                pl.BlockSpec((self.tm, width), lambda i: (self.ctx_block(i), 0))]


def _pick(i, rows, lat_ref, ctx_ref):
    return jnp.where(rows.is_lat(i), lat_ref[...], ctx_ref[...])


def _ada_kernel(c_ref, w_ref, b_ref, o_ref):
    a = _silu(c_ref[...]).astype(BF16)
    o_ref[...] = _dot(a, w_ref[...].astype(BF16)) + b_ref[...]


def _ada_call(cond, w_ada, b_ada):
    depth, d, d6 = w_ada.shape
    tn = 1536
    return pl.pallas_call(
        _ada_kernel,
        grid=(depth, d6 // tn),
        in_specs=[
            pl.BlockSpec((MOD_ROWS, d), lambda l, j: (0, 0)),
            pl.BlockSpec((None, d, tn), lambda l, j: (l, 0, j)),
            pl.BlockSpec((None, 1, tn), lambda l, j: (l, 0, j)),
        ],
        out_specs=pl.BlockSpec((None, MOD_ROWS, tn), lambda l, j: (l, 0, j)),
        out_shape=jax.ShapeDtypeStruct((depth, MOD_ROWS, d6), F32),
        compiler_params=_cparams(2),
        name="ada_mod",
    )(cond, w_ada, b_ada.reshape(depth, 1, d6))


def _rope64_store(y, cos, sin, first_half, mul, dst):
    for j in range(y.shape[1] // LANES):
        yj = y[:, j * LANES:(j + 1) * LANES]
        sw = jnp.where(first_half, pltpu.roll(yj, LANES - 32, 1), pltpu.roll(yj, 32, 1))
        dst[:, j * LANES:(j + 1) * LANES] = ((yj * cos + sw * sin) * mul).astype(BF16)


def _pre_hidden(rows, s_refs, mod_ref, g_ref):
    x = s_refs[0][...] if len(s_refs) == 1 else _pick(pl.program_id(0), rows, *s_refs)
    return _modulate(x, g_ref[...], mod_ref[0:1, :], mod_ref[1:2, :]).astype(BF16)


def _pre_da_kernel(*refs, rows, n_streams):
    s_refs, (mod_ref, g_ref, w_ref, cos_ref, sin_ref, q_ref, k_ref, v_ref) = refs[:n_streams], refs[n_streams:]
    d = w_ref.shape[0]
    h = _pre_hidden(rows, s_refs, mod_ref, g_ref)
    cos = cos_ref[...]
    sin = sin_ref[...]
    first_half = (lax.broadcasted_iota(jnp.int32, cos.shape, 1) % 64) < 32
    _rope64_store(_dot(h, w_ref[:, 0:d]), cos, sin, first_half, QK_SCALE, q_ref)
    _rope64_store(_dot(h, w_ref[:, d:2 * d]), cos, sin, first_half, 1.0, k_ref)
    v_ref[...] = _dot(h, w_ref[:, 2 * d:3 * d]).astype(BF16)


def _pre_na_kernel(*refs, rows, n_streams):
    s_refs, (mod_ref, g_ref, w_ref, q_ref, k_ref, v_ref) = refs[:n_streams], refs[n_streams:]
    d = w_ref.shape[0]
    h = _pre_hidden(rows, s_refs, mod_ref, g_ref)
    q_ref[...] = (_dot(h, w_ref[:, 0:d]) * QK_SCALE).astype(BF16)
    k_ref[...] = _dot(h, w_ref[:, d:2 * d]).astype(BF16)
    v_ref[...] = _dot(h, w_ref[:, 2 * d:3 * d]).astype(BF16)


def _pre_ret_kernel(*refs, rows, n_streams):
    s_refs, (mod_ref, g_ref, w_ref, cos_ref, sin_ref, q_ref, k_ref, v_ref, sg_ref) = refs[:n_streams], refs[n_streams:]
    d = w_ref.shape[0]
    h = _pre_hidden(rows, s_refs, mod_ref, g_ref)
    cos = cos_ref[...]
    sin = sin_ref[...]
    dk = d // RET_HEADS
    for dst, col0, mul in ((q_ref, 0, 1.0), (k_ref, d, dk ** -0.5)):
        y = _dot(h, w_ref[:, col0:col0 + d])
        for hh in range(RET_HEADS):
            a0 = hh * dk
            x1 = y[:, a0:a0 + LANES]
            x2 = y[:, a0 + LANES:a0 + 2 * LANES]
            dst[:, a0:a0 + LANES] = ((x1 * cos - x2 * sin) * mul).astype(BF16)
            dst[:, a0 + LANES:a0 + 2 * LANES] = ((x2 * cos + x1 * sin) * mul).astype(BF16)
    v_ref[...] = _dot(h, w_ref[:, 2 * d:4 * d]).astype(BF16)
    sg_ref[...] = _silu(_dot(h, w_ref[:, 4 * d:6 * d])).astype(BF16)


def _pre_call(kernel_fn, streams, mods, norm_g2, li, w, rope, out_widths, rows, name):
    d = w.shape[0]
    tm = rows.tm
    n = rows.n_lat + rows.n_ctx

    def rope_map(i):
        return (jnp.where(rows.is_lat(i), i % rows.tiles_per_batch, rows.tiles_per_batch), 0)

    in_specs = [pl.BlockSpec((tm, d), lambda i: (i, 0))] if len(streams) == 1 else rows.split_specs(d)
    in_specs += [
        pl.BlockSpec((None, None, 6, d), lambda i: (li, rows.mod_row(i), 0, 0)),
        pl.BlockSpec((None, 1, d), lambda i: (2 * li, 0, 0)),
        _resident(w.shape),
    ]
    args = list(streams) + [mods, norm_g2, w]
    if rope is not None:
        in_specs += [pl.BlockSpec((tm, LANES), rope_map)] * 2
        args += list(rope)
    return pl.pallas_call(
        functools.partial(kernel_fn, rows=rows, n_streams=len(streams)),
        grid=(n // tm,),
        in_specs=in_specs,
        out_specs=[pl.BlockSpec((tm, wd), lambda i: (i, 0)) for wd in out_widths],
        out_shape=[jax.ShapeDtypeStruct((n, wd), BF16) for wd in out_widths],
        compiler_params=_cparams(1),
        name=name,
    )(*args)


def _post_kernel(*refs, rows, n_a, n_streams, final):
    a_refs, s_refs = refs[:n_a], refs[n_a:n_a + n_streams]
    mod_ref, g_ref, wo_ref, win_ref, wout_ref = refs[n_a + n_streams:n_a + n_streams + 5]
    o_ref = refs[-1]
    i = pl.program_id(0)
    fh = wout_ref.shape[0]
    a = a_refs[0][...] if n_a == 1 else _pick(i, rows, *a_refs)
    s = s_refs[0][...] if n_streams == 1 else _pick(i, rows, *s_refs)
    x = s + mod_ref[2:3, :] * _dot(a, wo_ref[...])
    h = _modulate(x, g_ref[...], mod_ref[3:4, :], mod_ref[4:5, :]).astype(BF16)
    hm = _dot(h, win_ref[...])
    act = (_silu(hm[:, :fh]) * hm[:, fh:]).astype(BF16)
    x = x + mod_ref[5:6, :] * _dot(act, wout_ref[...])
    if final:
        o_ref[...] = _rms(x) * refs[-2][...]
    else:
        o_ref[...] = x


def _post_call(a_parts, streams, mods, norm_g2, li, w_o, w_in, w_out, final_g, rows, final):
    d = w_o.shape[1]
    tm = rows.tm
    n_rows = rows.n_lat if final else rows.n_lat + rows.n_ctx
    ka = a_parts[0].shape[1]
    in_specs = [pl.BlockSpec((tm, ka), lambda i: (i, 0))] if len(a_parts) == 1 else rows.split_specs(ka)
    in_specs += [pl.BlockSpec((tm, d), lambda i: (i, 0))] if len(streams) == 1 else rows.split_specs(d)
    in_specs += [
        pl.BlockSpec((None, None, 6, d), lambda i: (li, rows.mod_row(i), 0, 0)),
        pl.BlockSpec((None, 1, d), lambda i: (2 * li + 1, 0, 0)),
        _resident(w_o.shape),
        _resident(w_in.shape),
        _resident(w_out.shape),
    ]
    args = list(a_parts) + list(streams) + [mods, norm_g2, w_o, w_in, w_out]
    if final:
        in_specs.append(pl.BlockSpec((1, d), lambda i: (0, 0)))
        args.append(final_g.reshape(1, d))
    return pl.pallas_call(
        functools.partial(_post_kernel, rows=rows, n_a=len(a_parts), n_streams=len(streams), final=final),
        grid=(n_rows // tm,),
        in_specs=in_specs,
        out_specs=pl.BlockSpec((tm, d), lambda i: (i, 0)),
        out_shape=jax.ShapeDtypeStruct((n_rows, d), F32),
        compiler_params=_cparams(1),
        name="post_final" if final else "post",
    )(*args)


def _with_ones_column(v):
    extra = (lax.broadcasted_iota(jnp.int32, v.shape, 1) == 0).astype(F32).astype(BF16)
    return jnp.concatenate([v, extra], axis=1)


def _row_max(scores):
    m = scores[0].max(axis=-1, keepdims=True)
    for s in scores[1:]:
        m = jnp.maximum(m, s.max(axis=-1, keepdims=True))
    return m


def _normalised(r):
    return r[:, :LANES] / r[:, LANES:LANES + 1]


def _softmax_pv(scores, values):
    m = _row_max(scores)
    r = None
    for s, v in zip(scores, values):
        part = _dot(jnp.exp2(s - m).astype(BF16), v)
        r = part if r is None else r + part
    return _normalised(r)


def _lane_halves(q):
    q = q.astype(F32)
    lane = lax.broadcasted_iota(jnp.int32, q.shape, 1)
    return jnp.where(lane < 64, q, 0.0).astype(BF16), jnp.where(lane >= 64, q, 0.0).astype(BF16)


def _merge_halves(lo, hi):
    lane = lax.broadcasted_iota(jnp.int32, lo.shape, 1)
    return jnp.where(lane < 64, lo, hi)


def _da_kernel(*refs, lam_init, need_ctx):
    lam_ref, sub_ref, q0_ref, qa_ref, qb_ref = refs[:5]
    refs = refs[5:]
    if need_ctx:
        qz_ref, refs = refs[0], refs[1:]
    kx_ref, kz_ref, kxb_ref, kzb_ref, vx_ref, vz_ref, ox_ref = refs[:7]
    refs = refs[7:]
    if need_ctx:
        oz_ref, refs = refs[0], refs[1:]
    p0_ref, p1_ref, va_ref = refs

    u = pl.program_id(2)
    tq = qa_ref.shape[0]
    n_lat = kx_ref.shape[0]
    lp = lam_ref[...]
    lam = (jnp.exp(jnp.sum(lp[0:1, :] * lp[1:2, :], axis=-1, keepdims=True))
           - jnp.exp(jnp.sum(lp[2:3, :] * lp[3:4, :], axis=-1, keepdims=True)) + lam_init)

    def qk_softmax(q_ref, kx, kz, p_ref):
        for mi, qm in enumerate(_lane_halves(q_ref[...])):
            sx = _dot_nt(qm, kx[...])
            sz = _dot_nt(qm, kz[...])
            m = _row_max([sx, sz])
            p_ref[mi, :, :n_lat] = jnp.exp2(sx - m).astype(BF16)
            p_ref[mi, :, n_lat:] = jnp.exp2(sz - m).astype(BF16)

    def finish(maps):
        acc = maps[0] - lam * maps[1]
        return (_rms(acc) * sub_ref[...] * (1.0 - lam_init)).astype(BF16)

    def pv(p_ref):
        va = va_ref[...]
        return finish([_normalised(_dot(p_ref[mi], va)) for mi in range(2)])

    @pl.when(jnp.logical_and(jnp.logical_and(pl.program_id(0) == 0, pl.program_id(1) == 0), u == 0))
    def _():
        qk_softmax(q0_ref, kx_ref, kz_ref, p0_ref)

    @pl.when(u == 0)
    def _():
        va_ref[:n_lat, :] = _with_ones_column(vx_ref[...])
        va_ref[n_lat:, :] = _with_ones_column(vz_ref[...])
        if need_ctx:
            kz = kz_ref[...]
            vz = va_ref[n_lat:, :]
            oz_ref[...] = finish([_softmax_pv([_dot_nt(qm, kz)], [vz]) for qm in _lane_halves(qz_ref[...])])

    qk_softmax(qa_ref, kx_ref, kz_ref, p1_ref)
    ox_ref[:tq, :] = pv(p0_ref)
    qk_softmax(qb_ref, kxb_ref, kzb_ref, p0_ref)
    ox_ref[tq:, :] = pv(p1_ref)


def _attn_specs(n_head_blocks, n_batch, lat_len, ctx_len, tq, need_ctx, chain_heads):
    n_lat_tiles = lat_len // tq
    n_steps = n_lat_tiles // 2
    ctx_blk0 = n_batch * lat_len // ctx_len

    def ahead(h, b, u):
        if chain_heads:
            s = jnp.minimum(h * n_batch + b + 1, n_head_blocks * n_batch - 1)
            hn, bn = s // n_batch, s % n_batch
        else:
            hn, bn = h, jnp.minimum(b + 1, n_batch - 1)
        last = u == n_steps - 1
        return jnp.where(last, hn, h), jnp.where(last, bn, b), jnp.where(last, 0, 2 * u + 2)

    def qb_map(h, b, u):
        hn, bn, tn = ahead(h, b, u)
        return (bn * n_lat_tiles + tn, hn)

    q_specs = [
        pl.BlockSpec((tq, LANES), lambda h, b, u: (0, h)),
        pl.BlockSpec((tq, LANES), lambda h, b, u: (b * n_lat_tiles + 2 * u + 1, h)),
        pl.BlockSpec((tq, LANES), qb_map),
    ]
    if need_ctx:
        q_specs.append(pl.BlockSpec((ctx_len, LANES), lambda h, b, u: (ctx_blk0 + b, h)))
    lat_spec = pl.BlockSpec((lat_len, LANES), lambda h, b, u: (b, h))
    ctx_spec = pl.BlockSpec((ctx_len, LANES), lambda h, b, u: (ctx_blk0 + b, h))
    kv_specs = [
        lat_spec, ctx_spec,
        pl.BlockSpec((lat_len, LANES), lambda h, b, u: (ahead(h, b, u)[1], ahead(h, b, u)[0])),
        pl.BlockSpec((ctx_len, LANES), lambda h, b, u: (ctx_blk0 + ahead(h, b, u)[1], ahead(h, b, u)[0])),
        lat_spec, ctx_spec,
    ]
    out_specs = [pl.BlockSpec((2 * tq, LANES), lambda h, b, u: (b * n_steps + u, h))]
    if need_ctx:
        out_specs.append(pl.BlockSpec((ctx_len, LANES), lambda h, b, u: (b, h)))
    return q_specs, kv_specs, out_specs


def _da_call(q, k, v, lam_p, subln, lam_init, n_batch, lat_len, ctx_len, need_ctx):
    d = q.shape[1]
    tq = ctx_len
    assert lat_len % (2 * tq) == 0
    q_specs, kv_specs, out_specs = _attn_specs(DA_HEADS, n_batch, lat_len, ctx_len, tq, need_ctx, True)
    out_shape = [jax.ShapeDtypeStruct((n_batch * lat_len, d), BF16)]
    if need_ctx:
        out_shape.append(jax.ShapeDtypeStruct((n_batch * ctx_len, d), BF16))
    n_q = len(q_specs)
    return pl.pallas_call(
        functools.partial(_da_kernel, lam_init=lam_init, need_ctx=need_ctx),
        grid=(DA_HEADS, n_batch, lat_len // (2 * tq)),
        in_specs=[pl.BlockSpec(lam_p.shape, lambda h, b, u: (0, 0)),
                  pl.BlockSpec((1, LANES), lambda h, b, u: (0, 0))] + q_specs + kv_specs,
        out_specs=out_specs,
        out_shape=out_shape,
        scratch_shapes=[pltpu.VMEM((2, tq, lat_len + ctx_len), BF16),
                        pltpu.VMEM((2, tq, lat_len + ctx_len), BF16),
                        pltpu.VMEM((lat_len + ctx_len, 2 * LANES), BF16)],
        compiler_params=_cparams(3, ordered=True),
        name="diff_attn",
    )(lam_p, subln.reshape(1, LANES), *([q] * n_q), k, k, k, k, v, v)


def _ret_kernel(dec_ref, qx_ref, qz_ref, kx_ref, kz_ref, vx_ref, vz_ref, gx_ref, gz_ref, ox_ref, oz_ref,
                acc_ref, dsb_ref):
    c = RET_CHUNK
    n_chunks = qx_ref.shape[0] // c
    head = pl.program_id(1)
    lgs = jax.nn.log_sigmoid(dec_ref[...])
    sel = lax.broadcasted_iota(jnp.int32, lgs.shape, 1) == head
    lgs = jnp.sum(jnp.where(sel, lgs, 0.0), axis=-1, keepdims=True)
    lg_f = lgs[0:1, :]
    lg_b = lgs[1:2, :]

    ii = lax.broadcasted_iota(jnp.int32, (c, c), 0)
    jj = lax.broadcasted_iota(jnp.int32, (c, c), 1)
    diff = (ii - jj).astype(F32)
    dmat = jnp.where(diff >= 0.0, jnp.exp(jnp.maximum(diff, 0.0) * lg_f), jnp.exp(jnp.maximum(-diff, 0.0) * lg_b))
    idx = lax.broadcasted_iota(jnp.int32, (c, 1), 0).astype(F32)
    xi_f = jnp.exp((idx + 1.0) * lg_f)
    xi_b = jnp.exp((c - idx) * lg_b)
    zeta_f = jnp.exp((c - 1.0 - idx) * lg_f)
    zeta_b = jnp.exp(idx * lg_b)
    decay_f = jnp.exp(c * lg_f)
    decay_b = jnp.exp(c * lg_b)

    def intra(q, k, v):
        return _dot((_dot_nt(q, k) * dmat).astype(BF16), v)

    def state_updates(k, v):
        k32 = k.astype(F32)
        return _dot_tn((k32 * zeta_f).astype(BF16), v), _dot_tn((k32 * zeta_b).astype(BF16), v)

    def finish(o, gate):
        return (_rms(o) * gate.astype(F32)).astype(BF16)

    qz, kz, vz = qz_ref[...], kz_ref[...], vz_ref[...]
    oz_ref[...] = finish(intra(qz, kz, vz), gz_ref[...])
    s_f, s_b = state_updates(kz, vz)

    for i in range(n_chunks):
        rows = slice(i * c, (i + 1) * c)
        q, k, v = qx_ref[rows, :], kx_ref[rows, :], vx_ref[rows, :]
        acc_ref[rows, :] = intra(q, k, v) + xi_f * _dot(q, s_f.astype(BF16))
        d_f, d_b = state_updates(k, v)
        dsb_ref[i] = d_b
        s_f = decay_f * s_f + d_f

    for i in reversed(range(n_chunks)):
        rows = slice(i * c, (i + 1) * c)
        o = acc_ref[rows, :] + xi_b * _dot(qx_ref[rows, :], s_b.astype(BF16))
        ox_ref[rows, :] = finish(o, gx_ref[rows, :])
        s_b = decay_b * s_b + dsb_ref[i]


def _ret_call(q, k, v, sg, decay, n_batch, lat_len, ctx_len):
    dq = q.shape[1]
    dv = v.shape[1]
    dkh = dq // RET_HEADS
    dvh = dv // RET_HEADS
    ctx_blk0 = n_batch * lat_len // ctx_len

    def lat(b, h):
        return (b, h)

    def ctx(b, h):
        return (ctx_blk0 + b, h)

    return pl.pallas_call(
        _ret_kernel,
        grid=(n_batch, RET_HEADS),
        in_specs=[
            pl.BlockSpec(decay.shape, lambda b, h: (0, 0)),
            pl.BlockSpec((lat_len, dkh), lat), pl.BlockSpec((ctx_len, dkh), ctx),
            pl.BlockSpec((lat_len, dkh), lat), pl.BlockSpec((ctx_len, dkh), ctx),
            pl.BlockSpec((lat_len, dvh), lat), pl.BlockSpec((ctx_len, dvh), ctx),
            pl.BlockSpec((lat_len, dvh), lat), pl.BlockSpec((ctx_len, dvh), ctx),
        ],
        out_specs=[pl.BlockSpec((lat_len, dvh), lat), pl.BlockSpec((ctx_len, dvh), lambda b, h: (b, h))],
        out_shape=[jax.ShapeDtypeStruct((n_batch * lat_len, dv), BF16),
                   jax.ShapeDtypeStruct((n_batch * ctx_len, dv), BF16)],
        scratch_shapes=[pltpu.VMEM((lat_len, dvh), F32),
                        pltpu.VMEM((lat_len // RET_CHUNK, dkh, dvh), F32)],
        compiler_params=_cparams(2),
        name="retention",
    )(decay, q, q, k, k, v, v, sg, sg)


def _na_tile_plan(n_rows, rows_per_tile):
    kh = min(NA_KH, n_rows)
    span = kh + rows_per_tile
    classes, tile_class = [], []
    for t in range(n_rows // rows_per_tile):
        r0 = t * rows_per_tile
        start = min(max(r0 - kh // 2, 0), n_rows - span)
        sig = []
        for r in range(r0, r0 + rows_per_tile):
            rs = min(max(r - kh // 2, 0), n_rows - kh)
            assert start <= rs and rs + kh <= start + span
            sig.append((rs - start, start - r + NA_KH - 1))
        sig = tuple(sig)
        if sig not in classes:
            classes.append(sig)
        tile_class.append(classes.index(sig))
    return tile_class, classes


def _na_kernel(rpb_ref, q0_ref, qa_ref, qb_ref, qz_ref, kx_ref, kz_ref, kxb_ref, kzb_ref, vx_ref, vz_ref,
               ox_ref, oz_ref, tdup_ref, bias_ref, p0_ref, p1_ref, vxa_ref, vza_ref, *, n_rows):
    pair = pl.program_id(0)
    b = pl.program_id(1)
    u = pl.program_id(2)
    w = GRID_W
    tq = qa_ref.shape[0]
    rows_per_tile = tq // w
    n_lat_tiles = n_rows // rows_per_tile
    kh = min(NA_KH, n_rows)
    span = kh + rows_per_tile
    n_win = span * w
    tile_class, classes = _na_tile_plan(n_rows, rows_per_tile)
    n_dr = 2 * NA_KH - 1
    n_dc = 2 * NA_KW - 1
    neg_inf = jnp.full((w, LANES), -jnp.inf, F32)

    @pl.when(jnp.logical_and(b == 0, u == 0))
    def _():
        col_q = lax.broadcasted_iota(jnp.int32, (w, LANES), 0)
        lane = lax.broadcasted_iota(jnp.int32, (w, LANES), 1)
        col_k = lane % w
        dc = col_k - col_q + (NA_KW - 1)
        cs = jnp.clip(col_q - NA_KW // 2, 0, w - NA_KW)
        in_cols = jnp.logical_and(col_k >= cs, col_k < cs + NA_KW)
        for hh in range(2):
            for dr in range(n_dr):
                base = ((pair * 2 + hh) * n_dr + dr) * n_dc
                toe = lax.fori_loop(0, n_dc, lambda i, acc: jnp.where(dc == i, rpb_ref[base + i], acc), neg_inf)
                tdup_ref[hh, dr] = jnp.where(in_cols, toe * LOG2E, -jnp.inf)
        for hh in range(2):
            for ci, sig in enumerate(classes):
                for rr, (first, dr0) in enumerate(sig):
                    for j in range(n_win // LANES):
                        halves = []
                        for a in (2 * j, 2 * j + 1):
                            ok = first <= a < first + kh
                            halves.append(tdup_ref[hh, dr0 + a] if ok else neg_inf)
                        bias_ref[hh, ci, rr * w:(rr + 1) * w, j * LANES:(j + 1) * LANES] = (
                            jnp.where(lane < w, halves[0], halves[1]))

    def window_start(t):
        if isinstance(t, int):
            return min(max(t * rows_per_tile - kh // 2, 0), n_rows - span) * w
        start_row = jnp.clip(t * rows_per_tile - kh // 2, 0, n_rows - span)
        return pl.multiple_of(start_row * w, w)

    def tile_class_of(t):
        if isinstance(t, int):
            return tile_class[t]
        cls = jnp.int32(tile_class[0])
        for tt in range(1, n_lat_tiles):
            cls = jnp.where(t == tt, tile_class[tt], cls)
        return cls

    def qk_softmax(q_ref, kx, kz, t, p_ref):
        kwin = kx[pl.ds(window_start(t), n_win), :]
        kz = kz[...]
        cls = tile_class_of(t)
        for hh, qm in enumerate(_lane_halves(q_ref[...])):
            s_lat = _dot_nt(qm, kwin) + bias_ref[hh, cls]
            s_ctx = _dot_nt(qm, kz)
            m = _row_max([s_lat, s_ctx])
            p_ref[hh, :, :n_win] = jnp.exp2(s_lat - m).astype(BF16)
            p_ref[hh, :, n_win:] = jnp.exp2(s_ctx - m).astype(BF16)

    def pv(t, p_ref):
        vwin = vxa_ref[pl.ds(window_start(t), n_win), :]
        vz = vza_ref[...]
        outs = [_normalised(_dot(p_ref[hh, :, :n_win], vwin) + _dot(p_ref[hh, :, n_win:], vz)) for hh in range(2)]
        return _merge_halves(*outs).astype(BF16)

    @pl.when(jnp.logical_and(b == 0, u == 0))
    def _():
        qk_softmax(q0_ref, kx_ref, kz_ref, 0, p0_ref)

    @pl.when(u == 0)
    def _():
        vxa_ref[...] = _with_ones_column(vx_ref[...])
        vza_ref[...] = _with_ones_column(vz_ref[...])
        kz = kz_ref[...]
        vz = vza_ref[...]
        outs = [_softmax_pv([_dot_nt(qm, kz)], [vz]) for qm in _lane_halves(qz_ref[...])]
        oz_ref[...] = _merge_halves(*outs).astype(BF16)

    qk_softmax(qa_ref, kx_ref, kz_ref, 2 * u + 1, p1_ref)
    ox_ref[:tq, :] = pv(2 * u, p0_ref)
    qk_softmax(qb_ref, kxb_ref, kzb_ref, jnp.where(u == n_lat_tiles // 2 - 1, 0, 2 * u + 2), p0_ref)
    ox_ref[tq:, :] = pv(2 * u + 1, p1_ref)


def _na_call(q, k, v, rpb, n_batch, lat_len, ctx_len):
    d = q.shape[1]
    tq = ctx_len
    assert lat_len % (2 * tq) == 0 and tq % GRID_W == 0
    n_rows = lat_len // GRID_W
    rows_per_tile = tq // GRID_W
    n_win = (min(NA_KH, n_rows) + rows_per_tile) * GRID_W
    n_classes = len(_na_tile_plan(n_rows, rows_per_tile)[1])
    q_specs, kv_specs, out_specs = _attn_specs(NA_HEADS // 2, n_batch, lat_len, ctx_len, tq, True, False)
    return pl.pallas_call(
        functools.partial(_na_kernel, n_rows=n_rows),
        grid=(NA_HEADS // 2, n_batch, lat_len // (2 * tq)),
        in_specs=[pl.BlockSpec(memory_space=pltpu.SMEM)] + q_specs + kv_specs,
        out_specs=out_specs,
        out_shape=[jax.ShapeDtypeStruct((n_batch * lat_len, d), BF16),
                   jax.ShapeDtypeStruct((n_batch * ctx_len, d), BF16)],
        scratch_shapes=[pltpu.VMEM((2, 2 * NA_KH - 1, GRID_W, LANES), F32),
                        pltpu.VMEM((2, n_classes, tq, n_win), F32),
                        pltpu.VMEM((2, tq, n_win + ctx_len), BF16),
                        pltpu.VMEM((2, tq, n_win + ctx_len), BF16),
                        pltpu.VMEM((lat_len, 2 * LANES), BF16),
                        pltpu.VMEM((ctx_len, 2 * LANES), BF16)],
        compiler_params=_cparams(3, ordered=True),
        name="nbr_attn",
    )(rpb.astype(F32).reshape(-1), q, q, q, q, k, k, k, k, v, v)


def _rope_tables(lat_len, head_dim, tm):
    f = head_dim // 4
    t = jnp.arange(lat_len)
    inv = ROPE_BASE ** (-jnp.arange(f, dtype=F32) / f)
    ang = jnp.concatenate([(t // GRID_W).astype(F32)[:, None] * inv, (t % GRID_W).astype(F32)[:, None] * inv], axis=-1)
    cos, sin = jnp.cos(ang), jnp.sin(ang)
    if head_dim // 2 < LANES:
        reps = LANES // head_dim
        cos = jnp.tile(jnp.concatenate([cos, cos], axis=-1), (1, reps))
        sin = jnp.tile(jnp.concatenate([-sin, sin], axis=-1), (1, reps))
    cos = jnp.concatenate([cos, jnp.ones((tm, LANES), F32)], axis=0)
    sin = jnp.concatenate([sin, jnp.zeros((tm, LANES), F32)], axis=0)
    return cos, sin


def kernel(x, c, ctx, c_ctx, w_ada, b_ada, norm_g, ffn_in, ffn_out, final_g, da_w_qkv, da_w_o, da_lambda, da_subln,
           ret_w_in, ret_w_o, ret_decay, na_w_qkv, na_w_o, na_rpb):
    n_batch, lat_len, d = x.shape
    ctx_len = ctx.shape[1]
    assert n_batch + 1 <= MOD_ROWS
    rows = _Rows(n_batch * lat_len, lat_len, n_batch * ctx_len, tm=512)

    cond = jnp.concatenate([c, c_ctx[None, :], jnp.zeros((MOD_ROWS - n_batch - 1, d), F32)], axis=0)
    mods = _ada_call(cond, w_ada, b_ada).reshape(DEPTH, MOD_ROWS, 6, d)
    norm_g2 = norm_g.reshape(2 * DEPTH, 1, d)
    streams = (x.reshape(rows.n_lat, d), ctx.reshape(rows.n_ctx, d))

    rope64 = _rope_tables(lat_len, d // (2 * DA_HEADS), rows.tm)
    rope256 = _rope_tables(lat_len, d // RET_HEADS, rows.tm)

    ia = ib = ic = 0
    for li in range(DEPTH):
        kind = li % 3
        final = li == DEPTH - 1
        if kind == 0:
            q, k, v = _pre_call(_pre_da_kernel, streams, mods, norm_g2, li, da_w_qkv[ia].astype(BF16), rope64,
                                (d, d, d), rows, "pre_diff_attn")
            lam_init = 0.8 - 0.6 * math.exp(-0.3 * li)
            a_parts = _da_call(q, k, v, da_lambda[ia], da_subln[ia], lam_init, n_batch, lat_len, ctx_len, not final)
            w_o = da_w_o[ia]
            ia += 1
        elif kind == 1:
            q, k, v, sg = _pre_call(_pre_ret_kernel, streams, mods, norm_g2, li, ret_w_in[ib].astype(BF16), rope256,
                                    (d, d, 2 * d, 2 * d), rows, "pre_retention")
            a_parts = _ret_call(q, k, v, sg, ret_decay[ib], n_batch, lat_len, ctx_len)
            w_o = ret_w_o[ib]
            ib += 1
        else:
            q, k, v = _pre_call(_pre_na_kernel, streams, mods, norm_g2, li, na_w_qkv[ic].astype(BF16), None,
                                (d, d, d), rows, "pre_nbr_attn")
            a_parts = _na_call(q, k, v, na_rpb[ic], n_batch, lat_len, ctx_len)
            w_o = na_w_o[ic]
            ic += 1
        streams = (_post_call(a_parts, streams, mods, norm_g2, li, w_o.astype(BF16), ffn_in[li].astype(BF16),
                              ffn_out[li].astype(BF16), final_g, rows, final),)
    return streams[0].reshape(n_batch, lat_len, d)
```

```python
import functools
import math

import jax
import jax.numpy as jnp
from jax import lax
from jax.experimental import pallas as pl
from jax.experimental.pallas import tpu as pltpu

F32 = jnp.float32
BF16 = jnp.bfloat16

DEPTH = 4
GRID_W = 64
DA_HEADS = 8
RET_HEADS = 4
NA_HEADS = 16
NA_KH = 8
NA_KW = 16
ROPE_BASE = 10000.0
EPS = 1e-6
LOG2E = 1.4426950408889634
QK_SCALE = 0.125 * LOG2E
LANES = 128
RET_CHUNK = 256
ATTN_TILES_PER_STEP = 4
MOD_ROWS = 16
VMEM_LIMIT = 56 * 1024 * 1024


def _cparams(n_axes, ordered=False):
    sem = ("arbitrary" if ordered else "parallel",) * n_axes
    return pltpu.CompilerParams(dimension_semantics=sem, vmem_limit_bytes=VMEM_LIMIT)


def _resident(shape):
    nd = len(shape)
    return pl.BlockSpec(shape, lambda *_: (0,) * nd, pipeline_mode=pl.Buffered(1))


def _rms(x):
    return x * lax.rsqrt(jnp.mean(x * x, axis=-1, keepdims=True) + EPS)


def _modulate(x, g, shift, scale):
    return _rms(x) * g * (1.0 + scale) + shift


def _silu(x):
    return x * jax.nn.sigmoid(x)


def _dot(a, b):
    return jnp.dot(a, b, preferred_element_type=F32)


def _dot_nt(a, b):
    return lax.dot_general(a, b, (((1,), (1,)), ((), ())), preferred_element_type=F32)


def _dot_tn(a, b):
    return lax.dot_general(a, b, (((0,), (0,)), ((), ())), preferred_element_type=F32)


class _Rows:
    def __init__(self, n_lat, lat_len, n_ctx, tm):
        assert lat_len % tm == 0 and n_ctx % tm == 0
        self.n_lat, self.n_ctx, self.tm = n_lat, n_ctx, tm
        self.lat_tiles = n_lat // tm
        self.ctx_tiles = n_ctx // tm
        self.tiles_per_batch = lat_len // tm
        self.n_batch = n_lat // lat_len

    def is_lat(self, i):
        return i < self.lat_tiles

    def mod_row(self, i):
        return jnp.where(i < self.lat_tiles, i // self.tiles_per_batch, self.n_batch)

    def lat_block(self, i):
        return jnp.minimum(i, self.lat_tiles - 1)

    def ctx_block(self, i):
        return jnp.maximum(i - self.lat_tiles, 0)

    def split_specs(self, width):
        return [pl.BlockSpec((self.tm, width), lambda i: (self.lat_block(i), 0)),
                pl.BlockSpec((self.tm, width), lambda i: (self.ctx_block(i), 0))]


def _pick(i, rows, lat_ref, ctx_ref):
    return jnp.where(rows.is_lat(i), lat_ref[...], ctx_ref[...])


def _ada_kernel(c_ref, w_ref, b_ref, o_ref):
    a = _silu(c_ref[...]).astype(BF16)
    o_ref[...] = _dot(a, w_ref[...].astype(BF16)) + b_ref[...]


def _ada_call(cond, w_ada, b_ada):
    depth, d, d6 = w_ada.shape
    tn = 1536
    return pl.pallas_call(
        _ada_kernel,
        grid=(depth, d6 // tn),
        in_specs=[
            pl.BlockSpec((MOD_ROWS, d), lambda l, j: (0, 0)),
            pl.BlockSpec((None, d, tn), lambda l, j: (l, 0, j)),
            pl.BlockSpec((None, 1, tn), lambda l, j: (l, 0, j)),
        ],
        out_specs=pl.BlockSpec((None, MOD_ROWS, tn), lambda l, j: (l, 0, j)),
        out_shape=jax.ShapeDtypeStruct((depth, MOD_ROWS, d6), F32),
        compiler_params=_cparams(2),
        name="ada_mod",
    )(cond, w_ada, b_ada.reshape(depth, 1, d6))


def _rope64_store(y, cos, sin, first_half, mul, dst):
    for j in range(y.shape[1] // LANES):
        yj = y[:, j * LANES:(j + 1) * LANES]
        sw = jnp.where(first_half, pltpu.roll(yj, LANES - 32, 1), pltpu.roll(yj, 32, 1))
        dst[:, j * LANES:(j + 1) * LANES] = ((yj * cos + sw * sin) * mul).astype(BF16)


def _pre_hidden(rows, s_refs, mod_ref, g_ref):
    x = s_refs[0][...] if len(s_refs) == 1 else _pick(pl.program_id(0), rows, *s_refs)
    return _modulate(x, g_ref[...], mod_ref[0:1, :], mod_ref[1:2, :]).astype(BF16)


def _pre_da_kernel(*refs, rows, n_streams):
    s_refs, (mod_ref, g_ref, w_ref, cos_ref, sin_ref, q_ref, k_ref, v_ref) = refs[:n_streams], refs[n_streams:]
    d = w_ref.shape[0]
    h = _pre_hidden(rows, s_refs, mod_ref, g_ref)
    cos = cos_ref[...]
    sin = sin_ref[...]
    first_half = (lax.broadcasted_iota(jnp.int32, cos.shape, 1) % 64) < 32
    _rope64_store(_dot(h, w_ref[:, 0:d]), cos, sin, first_half, QK_SCALE, q_ref)
    _rope64_store(_dot(h, w_ref[:, d:2 * d]), cos, sin, first_half, 1.0, k_ref)
    v_ref[...] = _dot(h, w_ref[:, 2 * d:3 * d]).astype(BF16)


def _pre_na_kernel(*refs, rows, n_streams):
    s_refs, (mod_ref, g_ref, w_ref, q_ref, k_ref, v_ref) = refs[:n_streams], refs[n_streams:]
    d = w_ref.shape[0]
    h = _pre_hidden(rows, s_refs, mod_ref, g_ref)
    q_ref[...] = (_dot(h, w_ref[:, 0:d]) * QK_SCALE).astype(BF16)
    k_ref[...] = _dot(h, w_ref[:, d:2 * d]).astype(BF16)
    v_ref[...] = _dot(h, w_ref[:, 2 * d:3 * d]).astype(BF16)


def _pre_ret_kernel(*refs, rows, n_streams):
    s_refs, (mod_ref, g_ref, w_ref, cos_ref, sin_ref, q_ref, k_ref, v_ref, sg_ref) = refs[:n_streams], refs[n_streams:]
    d = w_ref.shape[0]
    h = _pre_hidden(rows, s_refs, mod_ref, g_ref)
    cos = cos_ref[...]
    sin = sin_ref[...]
    dk = d // RET_HEADS
    for dst, col0, mul in ((q_ref, 0, 1.0), (k_ref, d, dk ** -0.5)):
        y = _dot(h, w_ref[:, col0:col0 + d])
        for hh in range(RET_HEADS):
            a0 = hh * dk
            x1 = y[:, a0:a0 + LANES]
            x2 = y[:, a0 + LANES:a0 + 2 * LANES]
            dst[:, a0:a0 + LANES] = ((x1 * cos - x2 * sin) * mul).astype(BF16)
            dst[:, a0 + LANES:a0 + 2 * LANES] = ((x2 * cos + x1 * sin) * mul).astype(BF16)
    v_ref[...] = _dot(h, w_ref[:, 2 * d:4 * d]).astype(BF16)
    sg_ref[...] = _silu(_dot(h, w_ref[:, 4 * d:6 * d])).astype(BF16)


def _pre_call(kernel_fn, streams, mods, norm_g2, li, w, rope, out_widths, rows, name):
    d = w.shape[0]
    tm = rows.tm
    n = rows.n_lat + rows.n_ctx

    def rope_map(i):
        return (jnp.where(rows.is_lat(i), i % rows.tiles_per_batch, rows.tiles_per_batch), 0)

    in_specs = [pl.BlockSpec((tm, d), lambda i: (i, 0))] if len(streams) == 1 else rows.split_specs(d)
    in_specs += [
        pl.BlockSpec((None, None, 6, d), lambda i: (li, rows.mod_row(i), 0, 0)),
        pl.BlockSpec((None, 1, d), lambda i: (2 * li, 0, 0)),
        _resident(w.shape),
    ]
    args = list(streams) + [mods, norm_g2, w]
    if rope is not None:
        in_specs += [pl.BlockSpec((tm, LANES), rope_map)] * 2
        args += list(rope)
    return pl.pallas_call(
        functools.partial(kernel_fn, rows=rows, n_streams=len(streams)),
        grid=(n // tm,),
        in_specs=in_specs,
        out_specs=[pl.BlockSpec((tm, wd), lambda i: (i, 0)) for wd in out_widths],
        out_shape=[jax.ShapeDtypeStruct((n, wd), BF16) for wd in out_widths],
        compiler_params=_cparams(1),
        name=name,
    )(*args)


def _post_kernel(*refs, rows, n_a, n_streams, final):
    a_refs, s_refs = refs[:n_a], refs[n_a:n_a + n_streams]
    mod_ref, g_ref, wo_ref, win_ref, wout_ref = refs[n_a + n_streams:n_a + n_streams + 5]
    o_ref = refs[-1]
    i = pl.program_id(0)
    fh = wout_ref.shape[0]
    a = a_refs[0][...] if n_a == 1 else _pick(i, rows, *a_refs)
    s = s_refs[0][...] if n_streams == 1 else _pick(i, rows, *s_refs)
    x = s + mod_ref[2:3, :] * _dot(a, wo_ref[...])
    h = _modulate(x, g_ref[...], mod_ref[3:4, :], mod_ref[4:5, :]).astype(BF16)
    hm = _dot(h, win_ref[...])
    act = (_silu(hm[:, :fh]) * hm[:, fh:]).astype(BF16)
    x = x + mod_ref[5:6, :] * _dot(act, wout_ref[...])
    if final:
        o_ref[...] = _rms(x) * refs[-2][...]
    else:
        o_ref[...] = x


def _post_call(a_parts, streams, mods, norm_g2, li, w_o, w_in, w_out, final_g, rows, final):
    d = w_o.shape[1]
    tm = rows.tm
    n_rows = rows.n_lat if final else rows.n_lat + rows.n_ctx
    ka = a_parts[0].shape[1]
    in_specs = [pl.BlockSpec((tm, ka), lambda i: (i, 0))] if len(a_parts) == 1 else rows.split_specs(ka)
    in_specs += [pl.BlockSpec((tm, d), lambda i: (i, 0))] if len(streams) == 1 else rows.split_specs(d)
    in_specs += [
        pl.BlockSpec((None, None, 6, d), lambda i: (li, rows.mod_row(i), 0, 0)),
        pl.BlockSpec((None, 1, d), lambda i: (2 * li + 1, 0, 0)),
        _resident(w_o.shape),
        _resident(w_in.shape),
        _resident(w_out.shape),
    ]
    args = list(a_parts) + list(streams) + [mods, norm_g2, w_o, w_in, w_out]
    if final:
        in_specs.append(pl.BlockSpec((1, d), lambda i: (0, 0)))
        args.append(final_g.reshape(1, d))
    return pl.pallas_call(
        functools.partial(_post_kernel, rows=rows, n_a=len(a_parts), n_streams=len(streams), final=final),
        grid=(n_rows // tm,),
        in_specs=in_specs,
        out_specs=pl.BlockSpec((tm, d), lambda i: (i, 0)),
        out_shape=jax.ShapeDtypeStruct((n_rows, d), F32),
        compiler_params=_cparams(1),
        name="post_final" if final else "post",
    )(*args)


def _with_ones_column(v):
    extra = (lax.broadcasted_iota(jnp.int32, v.shape, 1) == 0).astype(F32).astype(BF16)
    return jnp.concatenate([v, extra], axis=1)


def _row_max(scores):
    m = scores[0].max(axis=-1, keepdims=True)
    for s in scores[1:]:
        m = jnp.maximum(m, s.max(axis=-1, keepdims=True))
    return m


def _normalised(r):
    return r[:, :LANES] / r[:, LANES:LANES + 1]


def _softmax_pv(scores, values):
    m = _row_max(scores)
    r = None
    for s, v in zip(scores, values):
        part = _dot(jnp.exp2(s - m).astype(BF16), v)
        r = part if r is None else r + part
    return _normalised(r)


def _lane_halves(q):
    q = q.astype(F32)
    lane = lax.broadcasted_iota(jnp.int32, q.shape, 1)
    return jnp.where(lane < 64, q, 0.0).astype(BF16), jnp.where(lane >= 64, q, 0.0).astype(BF16)


def _merge_halves(lo, hi):
    lane = lax.broadcasted_iota(jnp.int32, lo.shape, 1)
    return jnp.where(lane < 64, lo, hi)


def _da_kernel(*refs, lam_init, need_ctx):
    lam_ref, sub_ref, q0_ref = refs[:3]
    q_refs, refs = refs[3:3 + ATTN_TILES_PER_STEP], refs[3 + ATTN_TILES_PER_STEP:]
    qa_ref = q_refs[0]
    if need_ctx:
        qz_ref, refs = refs[0], refs[1:]
    kx_ref, kz_ref, kxb_ref, kzb_ref, vx_ref, vz_ref, ox_ref = refs[:7]
    refs = refs[7:]
    if need_ctx:
        oz_ref, refs = refs[0], refs[1:]
    p0_ref, p1_ref, va_ref = refs

    u = pl.program_id(2)
    tq = qa_ref.shape[0]
    n_lat = kx_ref.shape[0]
    lp = lam_ref[...]
    lam = (jnp.exp(jnp.sum(lp[0:1, :] * lp[1:2, :], axis=-1, keepdims=True))
           - jnp.exp(jnp.sum(lp[2:3, :] * lp[3:4, :], axis=-1, keepdims=True)) + lam_init)

    def qk_softmax(q_ref, kx, kz, p_ref):
        for mi, qm in enumerate(_lane_halves(q_ref[...])):
            sx = _dot_nt(qm, kx[...])
            sz = _dot_nt(qm, kz[...])
            m = _row_max([sx, sz])
            p_ref[mi, :, :n_lat] = jnp.exp2(sx - m).astype(BF16)
            p_ref[mi, :, n_lat:] = jnp.exp2(sz - m).astype(BF16)

    def finish(maps):
        acc = maps[0] - lam * maps[1]
        return (_rms(acc) * sub_ref[...] * (1.0 - lam_init)).astype(BF16)

    def pv(p_ref):
        va = va_ref[...]
        return finish([_normalised(_dot(p_ref[mi], va)) for mi in range(2)])

    @pl.when(jnp.logical_and(jnp.logical_and(pl.program_id(0) == 0, pl.program_id(1) == 0), u == 0))
    def _():
        qk_softmax(q0_ref, kx_ref, kz_ref, p0_ref)

    @pl.when(u == 0)
    def _():
        va_ref[:n_lat, :] = _with_ones_column(vx_ref[...])
        va_ref[n_lat:, :] = _with_ones_column(vz_ref[...])
        if need_ctx:
            kz = kz_ref[...]
            vz = va_ref[n_lat:, :]
            oz_ref[...] = finish([_softmax_pv([_dot_nt(qm, kz)], [vz]) for qm in _lane_halves(qz_ref[...])])

    slots = (p0_ref, p1_ref)
    for j in range(ATTN_TILES_PER_STEP):
        keys = (kxb_ref, kzb_ref) if j == ATTN_TILES_PER_STEP - 1 else (kx_ref, kz_ref)
        qk_softmax(q_refs[j], *keys, slots[(j + 1) % 2])
        ox_ref[j * tq:(j + 1) * tq, :] = pv(slots[j % 2])


def _attn_specs(n_head_blocks, n_batch, lat_len, ctx_len, tq, need_ctx, chain_heads):
    n_lat_tiles = lat_len // tq
    per_step = ATTN_TILES_PER_STEP
    assert per_step % 2 == 0 and n_lat_tiles % per_step == 0
    n_steps = n_lat_tiles // per_step
    ctx_blk0 = n_batch * lat_len // ctx_len

    def ahead(h, b, u):
        if chain_heads:
            s = jnp.minimum(h * n_batch + b + 1, n_head_blocks * n_batch - 1)
            hn, bn = s // n_batch, s % n_batch
        else:
            hn, bn = h, jnp.minimum(b + 1, n_batch - 1)
        last = u == n_steps - 1
        return jnp.where(last, hn, h), jnp.where(last, bn, b), jnp.where(last, 0, per_step * (u + 1))

    def ahead_map(h, b, u):
        hn, bn, tn = ahead(h, b, u)
        return (bn * n_lat_tiles + tn, hn)

    def tile_map(j):
        return lambda h, b, u: (b * n_lat_tiles + per_step * u + j, h)

    q_specs = [pl.BlockSpec((tq, LANES), lambda h, b, u: (0, h))]
    q_specs += [pl.BlockSpec((tq, LANES), tile_map(j)) for j in range(1, per_step)]
    q_specs.append(pl.BlockSpec((tq, LANES), ahead_map))
    if need_ctx:
        q_specs.append(pl.BlockSpec((ctx_len, LANES), lambda h, b, u: (ctx_blk0 + b, h)))
    lat_spec = pl.BlockSpec((lat_len, LANES), lambda h, b, u: (b, h))
    ctx_spec = pl.BlockSpec((ctx_len, LANES), lambda h, b, u: (ctx_blk0 + b, h))
    kv_specs = [
        lat_spec, ctx_spec,
        pl.BlockSpec((lat_len, LANES), lambda h, b, u: (ahead(h, b, u)[1], ahead(h, b, u)[0])),
        pl.BlockSpec((ctx_len, LANES), lambda h, b, u: (ctx_blk0 + ahead(h, b, u)[1], ahead(h, b, u)[0])),
        lat_spec, ctx_spec,
    ]
    out_specs = [pl.BlockSpec((per_step * tq, LANES), lambda h, b, u: (b * n_steps + u, h))]
    if need_ctx:
        out_specs.append(pl.BlockSpec((ctx_len, LANES), lambda h, b, u: (b, h)))
    return q_specs, kv_specs, out_specs


def _da_call(q, k, v, lam_p, subln, lam_init, n_batch, lat_len, ctx_len, need_ctx):
    d = q.shape[1]
    tq = ctx_len
    q_specs, kv_specs, out_specs = _attn_specs(DA_HEADS, n_batch, lat_len, ctx_len, tq, need_ctx, True)
    out_shape = [jax.ShapeDtypeStruct((n_batch * lat_len, d), BF16)]
    if need_ctx:
        out_shape.append(jax.ShapeDtypeStruct((n_batch * ctx_len, d), BF16))
    n_q = len(q_specs)
    return pl.pallas_call(
        functools.partial(_da_kernel, lam_init=lam_init, need_ctx=need_ctx),
        grid=(DA_HEADS, n_batch, lat_len // (ATTN_TILES_PER_STEP * tq)),
        in_specs=[pl.BlockSpec(lam_p.shape, lambda h, b, u: (0, 0)),
                  pl.BlockSpec((1, LANES), lambda h, b, u: (0, 0))] + q_specs + kv_specs,
        out_specs=out_specs,
        out_shape=out_shape,
        scratch_shapes=[pltpu.VMEM((2, tq, lat_len + ctx_len), BF16),
                        pltpu.VMEM((2, tq, lat_len + ctx_len), BF16),
                        pltpu.VMEM((lat_len + ctx_len, 2 * LANES), BF16)],
        compiler_params=_cparams(3, ordered=True),
        name="diff_attn",
    )(lam_p, subln.reshape(1, LANES), *([q] * n_q), k, k, k, k, v, v)


def _ret_kernel(dec_ref, qx_ref, qz_ref, kx_ref, kz_ref, vx_ref, vz_ref, gx_ref, gz_ref, ox_ref, oz_ref,
                acc_ref, dsb_ref):
    c = RET_CHUNK
    n_chunks = qx_ref.shape[0] // c
    head = pl.program_id(1)
    lgs = jax.nn.log_sigmoid(dec_ref[...])
    sel = lax.broadcasted_iota(jnp.int32, lgs.shape, 1) == head
    lgs = jnp.sum(jnp.where(sel, lgs, 0.0), axis=-1, keepdims=True)
    lg_f = lgs[0:1, :]
    lg_b = lgs[1:2, :]

    ii = lax.broadcasted_iota(jnp.int32, (c, c), 0)
    jj = lax.broadcasted_iota(jnp.int32, (c, c), 1)
    diff = (ii - jj).astype(F32)
    dmat = jnp.where(diff >= 0.0, jnp.exp(jnp.maximum(diff, 0.0) * lg_f), jnp.exp(jnp.maximum(-diff, 0.0) * lg_b))
    idx = lax.broadcasted_iota(jnp.int32, (c, 1), 0).astype(F32)
    xi_f = jnp.exp((idx + 1.0) * lg_f)
    xi_b = jnp.exp((c - idx) * lg_b)
    zeta_f = jnp.exp((c - 1.0 - idx) * lg_f)
    zeta_b = jnp.exp(idx * lg_b)
    decay_f = jnp.exp(c * lg_f)
    decay_b = jnp.exp(c * lg_b)

    def intra(q, k, v):
        return _dot((_dot_nt(q, k) * dmat).astype(BF16), v)

    def state_updates(k, v):
        k32 = k.astype(F32)
        return _dot_tn((k32 * zeta_f).astype(BF16), v), _dot_tn((k32 * zeta_b).astype(BF16), v)

    def finish(o, gate):
        return (_rms(o) * gate.astype(F32)).astype(BF16)

    qz, kz, vz = qz_ref[...], kz_ref[...], vz_ref[...]
    oz_ref[...] = finish(intra(qz, kz, vz), gz_ref[...])
    s_f, s_b = state_updates(kz, vz)

    for i in range(n_chunks):
        rows = slice(i * c, (i + 1) * c)
        q, k, v = qx_ref[rows, :], kx_ref[rows, :], vx_ref[rows, :]
        acc_ref[rows, :] = intra(q, k, v) + xi_f * _dot(q, s_f.astype(BF16))
        d_f, d_b = state_updates(k, v)
        dsb_ref[i] = d_b
        s_f = decay_f * s_f + d_f

    for i in reversed(range(n_chunks)):
        rows = slice(i * c, (i + 1) * c)
        o = acc_ref[rows, :] + xi_b * _dot(qx_ref[rows, :], s_b.astype(BF16))
        ox_ref[rows, :] = finish(o, gx_ref[rows, :])
        s_b = decay_b * s_b + dsb_ref[i]


def _ret_call(q, k, v, sg, decay, n_batch, lat_len, ctx_len):
    dq = q.shape[1]
    dv = v.shape[1]
    dkh = dq // RET_HEADS
    dvh = dv // RET_HEADS
    ctx_blk0 = n_batch * lat_len // ctx_len

    def lat(b, h):
        return (b, h)

    def ctx(b, h):
        return (ctx_blk0 + b, h)

    return pl.pallas_call(
        _ret_kernel,
        grid=(n_batch, RET_HEADS),
        in_specs=[
            pl.BlockSpec(decay.shape, lambda b, h: (0, 0)),
            pl.BlockSpec((lat_len, dkh), lat), pl.BlockSpec((ctx_len, dkh), ctx),
            pl.BlockSpec((lat_len, dkh), lat), pl.BlockSpec((ctx_len, dkh), ctx),
            pl.BlockSpec((lat_len, dvh), lat), pl.BlockSpec((ctx_len, dvh), ctx),
            pl.BlockSpec((lat_len, dvh), lat), pl.BlockSpec((ctx_len, dvh), ctx),
        ],
        out_specs=[pl.BlockSpec((lat_len, dvh), lat), pl.BlockSpec((ctx_len, dvh), lambda b, h: (b, h))],
        out_shape=[jax.ShapeDtypeStruct((n_batch * lat_len, dv), BF16),
                   jax.ShapeDtypeStruct((n_batch * ctx_len, dv), BF16)],
        scratch_shapes=[pltpu.VMEM((lat_len, dvh), F32),
                        pltpu.VMEM((lat_len // RET_CHUNK, dkh, dvh), F32)],
        compiler_params=_cparams(2),
        name="retention",
    )(decay, q, q, k, k, v, v, sg, sg)


def _na_tile_plan(n_rows, rows_per_tile):
    kh = min(NA_KH, n_rows)
    span = kh + rows_per_tile
    classes, tile_class = [], []
    for t in range(n_rows // rows_per_tile):
        r0 = t * rows_per_tile
        start = min(max(r0 - kh // 2, 0), n_rows - span)
        sig = []
        for r in range(r0, r0 + rows_per_tile):
            rs = min(max(r - kh // 2, 0), n_rows - kh)
            assert start <= rs and rs + kh <= start + span
            sig.append((rs - start, start - r + NA_KH - 1))
        sig = tuple(sig)
        if sig not in classes:
            classes.append(sig)
        tile_class.append(classes.index(sig))
    return tile_class, classes


def _na_kernel(rpb_ref, q0_ref, *refs, n_rows):
    per_step = ATTN_TILES_PER_STEP
    q_refs, refs = refs[:per_step], refs[per_step:]
    (qz_ref, kx_ref, kz_ref, kxb_ref, kzb_ref, vx_ref, vz_ref,
     ox_ref, oz_ref, tdup_ref, bias_ref, p0_ref, p1_ref, vxa_ref, vza_ref) = refs
    qa_ref = q_refs[0]
    pair = pl.program_id(0)
    b = pl.program_id(1)
    u = pl.program_id(2)
    w = GRID_W
    tq = qa_ref.shape[0]
    rows_per_tile = tq // w
    n_lat_tiles = n_rows // rows_per_tile
    kh = min(NA_KH, n_rows)
    span = kh + rows_per_tile
    n_win = span * w
    tile_class, classes = _na_tile_plan(n_rows, rows_per_tile)
    n_dr = 2 * NA_KH - 1
    n_dc = 2 * NA_KW - 1
    neg_inf = jnp.full((w, LANES), -jnp.inf, F32)

    @pl.when(jnp.logical_and(b == 0, u == 0))
    def _():
        col_q = lax.broadcasted_iota(jnp.int32, (w, LANES), 0)
        lane = lax.broadcasted_iota(jnp.int32, (w, LANES), 1)
        col_k = lane % w
        dc = col_k - col_q + (NA_KW - 1)
        cs = jnp.clip(col_q - NA_KW // 2, 0, w - NA_KW)
        in_cols = jnp.logical_and(col_k >= cs, col_k < cs + NA_KW)
        for hh in range(2):
            for dr in range(n_dr):
                base = ((pair * 2 + hh) * n_dr + dr) * n_dc
                toe = lax.fori_loop(0, n_dc, lambda i, acc: jnp.where(dc == i, rpb_ref[base + i], acc), neg_inf)
                tdup_ref[hh, dr] = jnp.where(in_cols, toe * LOG2E, -jnp.inf)
        for hh in range(2):
            for ci, sig in enumerate(classes):
                for rr, (first, dr0) in enumerate(sig):
                    for j in range(n_win // LANES):
                        halves = []
                        for a in (2 * j, 2 * j + 1):
                            ok = first <= a < first + kh
                            halves.append(tdup_ref[hh, dr0 + a] if ok else neg_inf)
                        bias_ref[hh, ci, rr * w:(rr + 1) * w, j * LANES:(j + 1) * LANES] = (
                            jnp.where(lane < w, halves[0], halves[1]))

    def window_start(t):
        if isinstance(t, int):
            return min(max(t * rows_per_tile - kh // 2, 0), n_rows - span) * w
        start_row = jnp.clip(t * rows_per_tile - kh // 2, 0, n_rows - span)
        return pl.multiple_of(start_row * w, w)

    def tile_class_of(t):
        if isinstance(t, int):
            return tile_class[t]
        cls = jnp.int32(tile_class[0])
        for tt in range(1, n_lat_tiles):
            cls = jnp.where(t == tt, tile_class[tt], cls)
        return cls

    def qk_softmax(q_ref, kx, kz, t, p_ref):
        kwin = kx[pl.ds(window_start(t), n_win), :]
        kz = kz[...]
        cls = tile_class_of(t)
        for hh, qm in enumerate(_lane_halves(q_ref[...])):
            s_lat = _dot_nt(qm, kwin) + bias_ref[hh, cls]
            s_ctx = _dot_nt(qm, kz)
            m = _row_max([s_lat, s_ctx])
            p_ref[hh, :, :n_win] = jnp.exp2(s_lat - m).astype(BF16)
            p_ref[hh, :, n_win:] = jnp.exp2(s_ctx - m).astype(BF16)

    def pv(t, p_ref):
        vwin = vxa_ref[pl.ds(window_start(t), n_win), :]
        vz = vza_ref[...]
        outs = [_normalised(_dot(p_ref[hh, :, :n_win], vwin) + _dot(p_ref[hh, :, n_win:], vz)) for hh in range(2)]
        return _merge_halves(*outs).astype(BF16)

    @pl.when(jnp.logical_and(b == 0, u == 0))
    def _():
        qk_softmax(q0_ref, kx_ref, kz_ref, 0, p0_ref)

    @pl.when(u == 0)
    def _():
        vxa_ref[...] = _with_ones_column(vx_ref[...])
        vza_ref[...] = _with_ones_column(vz_ref[...])
        kz = kz_ref[...]
        vz = vza_ref[...]
        outs = [_softmax_pv([_dot_nt(qm, kz)], [vz]) for qm in _lane_halves(qz_ref[...])]
        oz_ref[...] = _merge_halves(*outs).astype(BF16)

    slots = (p0_ref, p1_ref)
    for j in range(per_step):
        t = per_step * u + j
        if j == per_step - 1:
            t_next = jnp.where(u == n_lat_tiles // per_step - 1, 0, t + 1)
            qk_softmax(q_refs[j], kxb_ref, kzb_ref, t_next, slots[(j + 1) % 2])
        else:
            qk_softmax(q_refs[j], kx_ref, kz_ref, t + 1, slots[(j + 1) % 2])
        ox_ref[j * tq:(j + 1) * tq, :] = pv(t, slots[j % 2])


def _na_call(q, k, v, rpb, n_batch, lat_len, ctx_len):
    d = q.shape[1]
    tq = ctx_len
    assert tq % GRID_W == 0
    n_rows = lat_len // GRID_W
    rows_per_tile = tq // GRID_W
    n_win = (min(NA_KH, n_rows) + rows_per_tile) * GRID_W
    n_classes = len(_na_tile_plan(n_rows, rows_per_tile)[1])
    q_specs, kv_specs, out_specs = _attn_specs(NA_HEADS // 2, n_batch, lat_len, ctx_len, tq, True, False)
    return pl.pallas_call(
        functools.partial(_na_kernel, n_rows=n_rows),
        grid=(NA_HEADS // 2, n_batch, lat_len // (ATTN_TILES_PER_STEP * tq)),
        in_specs=[pl.BlockSpec(memory_space=pltpu.SMEM)] + q_specs + kv_specs,
        out_specs=out_specs,
        out_shape=[jax.ShapeDtypeStruct((n_batch * lat_len, d), BF16),
                   jax.ShapeDtypeStruct((n_batch * ctx_len, d), BF16)],
        scratch_shapes=[pltpu.VMEM((2, 2 * NA_KH - 1, GRID_W, LANES), F32),
                        pltpu.VMEM((2, n_classes, tq, n_win), F32),
                        pltpu.VMEM((2, tq, n_win + ctx_len), BF16),
                        pltpu.VMEM((2, tq, n_win + ctx_len), BF16),
                        pltpu.VMEM((lat_len, 2 * LANES), BF16),
                        pltpu.VMEM((ctx_len, 2 * LANES), BF16)],
        compiler_params=_cparams(3, ordered=True),
        name="nbr_attn",
    )(rpb.astype(F32).reshape(-1), *([q] * len(q_specs)), k, k, k, k, v, v)


def _rope_tables(lat_len, head_dim, tm):
    f = head_dim // 4
    t = jnp.arange(lat_len)
    inv = ROPE_BASE ** (-jnp.arange(f, dtype=F32) / f)
    ang = jnp.concatenate([(t // GRID_W).astype(F32)[:, None] * inv, (t % GRID_W).astype(F32)[:, None] * inv], axis=-1)
    cos, sin = jnp.cos(ang), jnp.sin(ang)
    if head_dim // 2 < LANES:
        reps = LANES // head_dim
        cos = jnp.tile(jnp.concatenate([cos, cos], axis=-1), (1, reps))
        sin = jnp.tile(jnp.concatenate([-sin, sin], axis=-1), (1, reps))
    cos = jnp.concatenate([cos, jnp.ones((tm, LANES), F32)], axis=0)
    sin = jnp.concatenate([sin, jnp.zeros((tm, LANES), F32)], axis=0)
    return cos, sin


def kernel(x, c, ctx, c_ctx, w_ada, b_ada, norm_g, ffn_in, ffn_out, final_g, da_w_qkv, da_w_o, da_lambda, da_subln,
           ret_w_in, ret_w_o, ret_decay, na_w_qkv, na_w_o, na_rpb):
    n_batch, lat_len, d = x.shape
    ctx_len = ctx.shape[1]
    assert n_batch + 1 <= MOD_ROWS
    rows = _Rows(n_batch * lat_len, lat_len, n_batch * ctx_len, tm=512)

    cond = jnp.concatenate([c, c_ctx[None, :], jnp.zeros((MOD_ROWS - n_batch - 1, d), F32)], axis=0)
    mods = _ada_call(cond, w_ada, b_ada).reshape(DEPTH, MOD_ROWS, 6, d)
    norm_g2 = norm_g.reshape(2 * DEPTH, 1, d)
    streams = (x.reshape(rows.n_lat, d), ctx.reshape(rows.n_ctx, d))

    rope64 = _rope_tables(lat_len, d // (2 * DA_HEADS), rows.tm)
    rope256 = _rope_tables(lat_len, d // RET_HEADS, rows.tm)

    ia = ib = ic = 0
    for li in range(DEPTH):
        kind = li % 3
        final = li == DEPTH - 1
        if kind == 0:
            q, k, v = _pre_call(_pre_da_kernel, streams, mods, norm_g2, li, da_w_qkv[ia].astype(BF16), rope64,
                                (d, d, d), rows, "pre_diff_attn")
            lam_init = 0.8 - 0.6 * math.exp(-0.3 * li)
            a_parts = _da_call(q, k, v, da_lambda[ia], da_subln[ia], lam_init, n_batch, lat_len, ctx_len, not final)
            w_o = da_w_o[ia]
            ia += 1
        elif kind == 1:
            q, k, v, sg = _pre_call(_pre_ret_kernel, streams, mods, norm_g2, li, ret_w_in[ib].astype(BF16), rope256,
                                    (d, d, 2 * d, 2 * d), rows, "pre_retention")
            a_parts = _ret_call(q, k, v, sg, ret_decay[ib], n_batch, lat_len, ctx_len)
            w_o = ret_w_o[ib]
            ib += 1
        else:
            q, k, v = _pre_call(_pre_na_kernel, streams, mods, norm_g2, li, na_w_qkv[ic].astype(BF16), None,
                                (d, d, d), rows, "pre_nbr_attn")
            a_parts = _na_call(q, k, v, na_rpb[ic], n_batch, lat_len, ctx_len)
            w_o = na_w_o[ic]
            ic += 1
        streams = (_post_call(a_parts, streams, mods, norm_g2, li, w_o.astype(BF16), ffn_in[li].astype(BF16),
                              ffn_out[li].astype(BF16), final_g, rows, final),)
    return streams[0].reshape(n_batch, lat_len, d)
```

```python
import functools
import math

import jax
import jax.numpy as jnp
from jax import lax
from jax.experimental import pallas as pl
from jax.experimental.pallas import tpu as pltpu

F32 = jnp.float32
BF16 = jnp.bfloat16

DEPTH = 4
GRID_W = 64
DA_HEADS = 8
RET_HEADS = 4
NA_HEADS = 16
NA_KH = 8
NA_KW = 16
ROPE_BASE = 10000.0
EPS = 1e-6
LOG2E = 1.4426950408889634
QK_SCALE = 0.125 * LOG2E
LANES = 128
RET_CHUNK = 256
ATTN_TILES_PER_STEP = 8
MOD_ROWS = 16
VMEM_LIMIT = 56 * 1024 * 1024


def _cparams(n_axes, ordered=False):
    sem = ("arbitrary" if ordered else "parallel",) * n_axes
    return pltpu.CompilerParams(dimension_semantics=sem, vmem_limit_bytes=VMEM_LIMIT)


def _resident(shape):
    nd = len(shape)
    return pl.BlockSpec(shape, lambda *_: (0,) * nd, pipeline_mode=pl.Buffered(1))


def _rms(x):
    return x * lax.rsqrt(jnp.mean(x * x, axis=-1, keepdims=True) + EPS)


def _modulate(x, g, shift, scale):
    return _rms(x) * g * (1.0 + scale) + shift


def _silu(x):
    return x * jax.nn.sigmoid(x)


def _dot(a, b):
    return jnp.dot(a, b, preferred_element_type=F32)


def _dot_nt(a, b):
    return lax.dot_general(a, b, (((1,), (1,)), ((), ())), preferred_element_type=F32)


def _dot_tn(a, b):
    return lax.dot_general(a, b, (((0,), (0,)), ((), ())), preferred_element_type=F32)


class _Rows:
    def __init__(self, n_lat, lat_len, n_ctx, tm):
        assert lat_len % tm == 0 and n_ctx % tm == 0
        self.n_lat, self.n_ctx, self.tm = n_lat, n_ctx, tm
        self.lat_tiles = n_lat // tm
        self.ctx_tiles = n_ctx // tm
        self.tiles_per_batch = lat_len // tm
        self.n_batch = n_lat // lat_len

    def is_lat(self, i):
        return i < self.lat_tiles

    def mod_row(self, i):
        return jnp.where(i < self.lat_tiles, i // self.tiles_per_batch, self.n_batch)

    def lat_block(self, i):
        return jnp.minimum(i, self.lat_tiles - 1)

    def ctx_block(self, i):
        return jnp.maximum(i - self.lat_tiles, 0)

    def split_specs(self, width):
        return [pl.BlockSpec((self.tm, width), lambda i: (self.lat_block(i), 0)),
                pl.BlockSpec((self.tm, width), lambda i: (self.ctx_block(i), 0))]


def _pick(i, rows, lat_ref, ctx_ref):
    return jnp.where(rows.is_lat(i), lat_ref[...], ctx_ref[...])


def _ada_kernel(c_ref, w_ref, b_ref, o_ref):
    a = _silu(c_ref[...]).astype(BF16)
    o_ref[...] = _dot(a, w_ref[...].astype(BF16)) + b_ref[...]


def _ada_call(cond, w_ada, b_ada):
    depth, d, d6 = w_ada.shape
    tn = 1536
    return pl.pallas_call(
        _ada_kernel,
        grid=(depth, d6 // tn),
        in_specs=[
            pl.BlockSpec((MOD_ROWS, d), lambda l, j: (0, 0)),
            pl.BlockSpec((None, d, tn), lambda l, j: (l, 0, j)),
            pl.BlockSpec((None, 1, tn), lambda l, j: (l, 0, j)),
        ],
        out_specs=pl.BlockSpec((None, MOD_ROWS, tn), lambda l, j: (l, 0, j)),
        out_shape=jax.ShapeDtypeStruct((depth, MOD_ROWS, d6), F32),
        compiler_params=_cparams(2),
        name="ada_mod",
    )(cond, w_ada, b_ada.reshape(depth, 1, d6))


def _rope64_store(y, cos, sin, first_half, mul, dst):
    for j in range(y.shape[1] // LANES):
        yj = y[:, j * LANES:(j + 1) * LANES]
        sw = jnp.where(first_half, pltpu.roll(yj, LANES - 32, 1), pltpu.roll(yj, 32, 1))
        dst[:, j * LANES:(j + 1) * LANES] = ((yj * cos + sw * sin) * mul).astype(BF16)


def _pre_hidden(rows, s_refs, mod_ref, g_ref):
    x = s_refs[0][...] if len(s_refs) == 1 else _pick(pl.program_id(0), rows, *s_refs)
    return _modulate(x, g_ref[...], mod_ref[0:1, :], mod_ref[1:2, :]).astype(BF16)


def _pre_da_kernel(*refs, rows, n_streams):
    s_refs, (mod_ref, g_ref, w_ref, cos_ref, sin_ref, q_ref, k_ref, v_ref) = refs[:n_streams], refs[n_streams:]
    d = w_ref.shape[0]
    h = _pre_hidden(rows, s_refs, mod_ref, g_ref)
    cos = cos_ref[...]
    sin = sin_ref[...]
    first_half = (lax.broadcasted_iota(jnp.int32, cos.shape, 1) % 64) < 32
    _rope64_store(_dot(h, w_ref[:, 0:d]), cos, sin, first_half, QK_SCALE, q_ref)
    _rope64_store(_dot(h, w_ref[:, d:2 * d]), cos, sin, first_half, 1.0, k_ref)
    v_ref[...] = _dot(h, w_ref[:, 2 * d:3 * d]).astype(BF16)


def _pre_na_kernel(*refs, rows, n_streams):
    s_refs, (mod_ref, g_ref, w_ref, q_ref, k_ref, v_ref) = refs[:n_streams], refs[n_streams:]
    d = w_ref.shape[0]
    h = _pre_hidden(rows, s_refs, mod_ref, g_ref)
    q_ref[...] = (_dot(h, w_ref[:, 0:d]) * QK_SCALE).astype(BF16)
    k_ref[...] = _dot(h, w_ref[:, d:2 * d]).astype(BF16)
    v_ref[...] = _dot(h, w_ref[:, 2 * d:3 * d]).astype(BF16)


def _pre_ret_kernel(*refs, rows, n_streams):
    s_refs, (mod_ref, g_ref, w_ref, cos_ref, sin_ref, q_ref, k_ref, v_ref, sg_ref) = refs[:n_streams], refs[n_streams:]
    d = w_ref.shape[0]
    h = _pre_hidden(rows, s_refs, mod_ref, g_ref)
    cos = cos_ref[...]
    sin = sin_ref[...]
    dk = d // RET_HEADS
    for dst, col0, mul in ((q_ref, 0, 1.0), (k_ref, d, dk ** -0.5)):
        y = _dot(h, w_ref[:, col0:col0 + d])
        for hh in range(RET_HEADS):
            a0 = hh * dk
            x1 = y[:, a0:a0 + LANES]
            x2 = y[:, a0 + LANES:a0 + 2 * LANES]
            dst[:, a0:a0 + LANES] = ((x1 * cos - x2 * sin) * mul).astype(BF16)
            dst[:, a0 + LANES:a0 + 2 * LANES] = ((x2 * cos + x1 * sin) * mul).astype(BF16)
    v_ref[...] = _dot(h, w_ref[:, 2 * d:4 * d]).astype(BF16)
    sg_ref[...] = _silu(_dot(h, w_ref[:, 4 * d:6 * d])).astype(BF16)


def _pre_call(kernel_fn, streams, mods, norm_g2, li, w, rope, out_widths, rows, name):
    d = w.shape[0]
    tm = rows.tm
    n = rows.n_lat + rows.n_ctx

    def rope_map(i):
        return (jnp.where(rows.is_lat(i), i % rows.tiles_per_batch, rows.tiles_per_batch), 0)

    in_specs = [pl.BlockSpec((tm, d), lambda i: (i, 0))] if len(streams) == 1 else rows.split_specs(d)
    in_specs += [
        pl.BlockSpec((None, None, 6, d), lambda i: (li, rows.mod_row(i), 0, 0)),
        pl.BlockSpec((None, 1, d), lambda i: (2 * li, 0, 0)),
        _resident(w.shape),
    ]
    args = list(streams) + [mods, norm_g2, w]
    if rope is not None:
        in_specs += [pl.BlockSpec((tm, LANES), rope_map)] * 2
        args += list(rope)
    return pl.pallas_call(
        functools.partial(kernel_fn, rows=rows, n_streams=len(streams)),
        grid=(n // tm,),
        in_specs=in_specs,
        out_specs=[pl.BlockSpec((tm, wd), lambda i: (i, 0)) for wd in out_widths],
        out_shape=[jax.ShapeDtypeStruct((n, wd), BF16) for wd in out_widths],
        compiler_params=_cparams(1),
        name=name,
    )(*args)


def _post_kernel(*refs, rows, n_a, n_streams, final):
    a_refs, s_refs = refs[:n_a], refs[n_a:n_a + n_streams]
    mod_ref, g_ref, wo_ref, win_ref, wout_ref = refs[n_a + n_streams:n_a + n_streams + 5]
    o_ref = refs[-1]
    i = pl.program_id(0)
    fh = wout_ref.shape[0]
    a = a_refs[0][...] if n_a == 1 else _pick(i, rows, *a_refs)
    s = s_refs[0][...] if n_streams == 1 else _pick(i, rows, *s_refs)
    x = s + mod_ref[2:3, :] * _dot(a, wo_ref[...])
    h = _modulate(x, g_ref[...], mod_ref[3:4, :], mod_ref[4:5, :]).astype(BF16)
    hm = _dot(h, win_ref[...])
    act = (_silu(hm[:, :fh]) * hm[:, fh:]).astype(BF16)
    x = x + mod_ref[5:6, :] * _dot(act, wout_ref[...])
    if final:
        o_ref[...] = _rms(x) * refs[-2][...]
    else:
        o_ref[...] = x


def _post_call(a_parts, streams, mods, norm_g2, li, w_o, w_in, w_out, final_g, rows, final):
    d = w_o.shape[1]
    tm = rows.tm
    n_rows = rows.n_lat if final else rows.n_lat + rows.n_ctx
    ka = a_parts[0].shape[1]
    in_specs = [pl.BlockSpec((tm, ka), lambda i: (i, 0))] if len(a_parts) == 1 else rows.split_specs(ka)
    in_specs += [pl.BlockSpec((tm, d), lambda i: (i, 0))] if len(streams) == 1 else rows.split_specs(d)
    in_specs += [
        pl.BlockSpec((None, None, 6, d), lambda i: (li, rows.mod_row(i), 0, 0)),
        pl.BlockSpec((None, 1, d), lambda i: (2 * li + 1, 0, 0)),
        _resident(w_o.shape),
        _resident(w_in.shape),
        _resident(w_out.shape),
    ]
    args = list(a_parts) + list(streams) + [mods, norm_g2, w_o, w_in, w_out]
    if final:
        in_specs.append(pl.BlockSpec((1, d), lambda i: (0, 0)))
        args.append(final_g.reshape(1, d))
    return pl.pallas_call(
        functools.partial(_post_kernel, rows=rows, n_a=len(a_parts), n_streams=len(streams), final=final),
        grid=(n_rows // tm,),
        in_specs=in_specs,
        out_specs=pl.BlockSpec((tm, d), lambda i: (i, 0)),
        out_shape=jax.ShapeDtypeStruct((n_rows, d), F32),
        compiler_params=_cparams(1),
        name="post_final" if final else "post",
    )(*args)


def _with_ones_column(v):
    extra = (lax.broadcasted_iota(jnp.int32, v.shape, 1) == 0).astype(F32).astype(BF16)
    return jnp.concatenate([v, extra], axis=1)


def _row_max(scores):
    m = scores[0].max(axis=-1, keepdims=True)
    for s in scores[1:]:
        m = jnp.maximum(m, s.max(axis=-1, keepdims=True))
    return m


def _normalised(r):
    return r[:, :LANES] / r[:, LANES:LANES + 1]


def _softmax_pv(scores, values):
    m = _row_max(scores)
    r = None
    for s, v in zip(scores, values):
        part = _dot(jnp.exp2(s - m).astype(BF16), v)
        r = part if r is None else r + part
    return _normalised(r)


def _lane_halves(q):
    q = q.astype(F32)
    lane = lax.broadcasted_iota(jnp.int32, q.shape, 1)
    return jnp.where(lane < 64, q, 0.0).astype(BF16), jnp.where(lane >= 64, q, 0.0).astype(BF16)


def _merge_halves(lo, hi):
    lane = lax.broadcasted_iota(jnp.int32, lo.shape, 1)
    return jnp.where(lane < 64, lo, hi)


def _da_kernel(*refs, lam_init, need_ctx):
    lam_ref, sub_ref, q0_ref = refs[:3]
    q_refs, refs = refs[3:3 + ATTN_TILES_PER_STEP], refs[3 + ATTN_TILES_PER_STEP:]
    qa_ref = q_refs[0]
    if need_ctx:
        qz_ref, refs = refs[0], refs[1:]
    kx_ref, kz_ref, kxb_ref, kzb_ref, vx_ref, vz_ref, ox_ref = refs[:7]
    refs = refs[7:]
    if need_ctx:
        oz_ref, refs = refs[0], refs[1:]
    p0_ref, p1_ref, va_ref = refs

    u = pl.program_id(2)
    tq = qa_ref.shape[0]
    n_lat = kx_ref.shape[0]
    lp = lam_ref[...]
    lam = (jnp.exp(jnp.sum(lp[0:1, :] * lp[1:2, :], axis=-1, keepdims=True))
           - jnp.exp(jnp.sum(lp[2:3, :] * lp[3:4, :], axis=-1, keepdims=True)) + lam_init)

    def qk_softmax(q_ref, kx, kz, p_ref):
        for mi, qm in enumerate(_lane_halves(q_ref[...])):
            sx = _dot_nt(qm, kx[...])
            sz = _dot_nt(qm, kz[...])
            m = _row_max([sx, sz])
            p_ref[mi, :, :n_lat] = jnp.exp2(sx - m).astype(BF16)
            p_ref[mi, :, n_lat:] = jnp.exp2(sz - m).astype(BF16)

    def finish(maps):
        acc = maps[0] - lam * maps[1]
        return (_rms(acc) * sub_ref[...] * (1.0 - lam_init)).astype(BF16)

    def pv(p_ref):
        va = va_ref[...]
        return finish([_normalised(_dot(p_ref[mi], va)) for mi in range(2)])

    @pl.when(jnp.logical_and(jnp.logical_and(pl.program_id(0) == 0, pl.program_id(1) == 0), u == 0))
    def _():
        qk_softmax(q0_ref, kx_ref, kz_ref, p0_ref)

    @pl.when(u == 0)
    def _():
        va_ref[:n_lat, :] = _with_ones_column(vx_ref[...])
        va_ref[n_lat:, :] = _with_ones_column(vz_ref[...])
        if need_ctx:
            kz = kz_ref[...]
            vz = va_ref[n_lat:, :]
            oz_ref[...] = finish([_softmax_pv([_dot_nt(qm, kz)], [vz]) for qm in _lane_halves(qz_ref[...])])

    slots = (p0_ref, p1_ref)
    for j in range(ATTN_TILES_PER_STEP):
        keys = (kxb_ref, kzb_ref) if j == ATTN_TILES_PER_STEP - 1 else (kx_ref, kz_ref)
        qk_softmax(q_refs[j], *keys, slots[(j + 1) % 2])
        ox_ref[j * tq:(j + 1) * tq, :] = pv(slots[j % 2])


def _attn_specs(n_head_blocks, n_batch, lat_len, ctx_len, tq, need_ctx, chain_heads):
    n_lat_tiles = lat_len // tq
    per_step = ATTN_TILES_PER_STEP
    assert per_step % 2 == 0 and n_lat_tiles % per_step == 0
    n_steps = n_lat_tiles // per_step
    ctx_blk0 = n_batch * lat_len // ctx_len

    def ahead(h, b, u):
        if chain_heads:
            s = jnp.minimum(h * n_batch + b + 1, n_head_blocks * n_batch - 1)
            hn, bn = s // n_batch, s % n_batch
        else:
            hn, bn = h, jnp.minimum(b + 1, n_batch - 1)
        last = u == n_steps - 1
        return jnp.where(last, hn, h), jnp.where(last, bn, b), jnp.where(last, 0, per_step * (u + 1))

    def ahead_map(h, b, u):
        hn, bn, tn = ahead(h, b, u)
        return (bn * n_lat_tiles + tn, hn)

    def tile_map(j):
        return lambda h, b, u: (b * n_lat_tiles + per_step * u + j, h)

    q_specs = [pl.BlockSpec((tq, LANES), lambda h, b, u: (0, h))]
    q_specs += [pl.BlockSpec((tq, LANES), tile_map(j)) for j in range(1, per_step)]
    q_specs.append(pl.BlockSpec((tq, LANES), ahead_map))
    if need_ctx:
        q_specs.append(pl.BlockSpec((ctx_len, LANES), lambda h, b, u: (ctx_blk0 + b, h)))
    lat_spec = pl.BlockSpec((lat_len, LANES), lambda h, b, u: (b, h))
    ctx_spec = pl.BlockSpec((ctx_len, LANES), lambda h, b, u: (ctx_blk0 + b, h))
    kv_specs = [
        lat_spec, ctx_spec,
        pl.BlockSpec((lat_len, LANES), lambda h, b, u: (ahead(h, b, u)[1], ahead(h, b, u)[0])),
        pl.BlockSpec((ctx_len, LANES), lambda h, b, u: (ctx_blk0 + ahead(h, b, u)[1], ahead(h, b, u)[0])),
        lat_spec, ctx_spec,
    ]
    out_specs = [pl.BlockSpec((per_step * tq, LANES), lambda h, b, u: (b * n_steps + u, h))]
    if need_ctx:
        out_specs.append(pl.BlockSpec((ctx_len, LANES), lambda h, b, u: (b, h)))
    return q_specs, kv_specs, out_specs


def _da_call(q, k, v, lam_p, subln, lam_init, n_batch, lat_len, ctx_len, need_ctx):
    d = q.shape[1]
    tq = ctx_len
    q_specs, kv_specs, out_specs = _attn_specs(DA_HEADS, n_batch, lat_len, ctx_len, tq, need_ctx, True)
    out_shape = [jax.ShapeDtypeStruct((n_batch * lat_len, d), BF16)]
    if need_ctx:
        out_shape.append(jax.ShapeDtypeStruct((n_batch * ctx_len, d), BF16))
    n_q = len(q_specs)
    return pl.pallas_call(
        functools.partial(_da_kernel, lam_init=lam_init, need_ctx=need_ctx),
        grid=(DA_HEADS, n_batch, lat_len // (ATTN_TILES_PER_STEP * tq)),
        in_specs=[pl.BlockSpec(lam_p.shape, lambda h, b, u: (0, 0)),
                  pl.BlockSpec((1, LANES), lambda h, b, u: (0, 0))] + q_specs + kv_specs,
        out_specs=out_specs,
        out_shape=out_shape,
        scratch_shapes=[pltpu.VMEM((2, tq, lat_len + ctx_len), BF16),
                        pltpu.VMEM((2, tq, lat_len + ctx_len), BF16),
                        pltpu.VMEM((lat_len + ctx_len, 2 * LANES), BF16)],
        compiler_params=_cparams(3, ordered=True),
        name="diff_attn",
    )(lam_p, subln.reshape(1, LANES), *([q] * n_q), k, k, k, k, v, v)


def _ret_kernel(dec_ref, qx_ref, qz_ref, kx_ref, kz_ref, vx_ref, vz_ref, gx_ref, gz_ref, ox_ref, oz_ref,
                acc_ref, dsb_ref):
    c = RET_CHUNK
    n_chunks = qx_ref.shape[0] // c
    head = pl.program_id(1)
    lgs = jax.nn.log_sigmoid(dec_ref[...])
    sel = lax.broadcasted_iota(jnp.int32, lgs.shape, 1) == head
    lgs = jnp.sum(jnp.where(sel, lgs, 0.0), axis=-1, keepdims=True)
    lg_f = lgs[0:1, :]
    lg_b = lgs[1:2, :]

    ii = lax.broadcasted_iota(jnp.int32, (c, c), 0)
    jj = lax.broadcasted_iota(jnp.int32, (c, c), 1)
    diff = (ii - jj).astype(F32)
    dmat = jnp.where(diff >= 0.0, jnp.exp(jnp.maximum(diff, 0.0) * lg_f), jnp.exp(jnp.maximum(-diff, 0.0) * lg_b))
    idx = lax.broadcasted_iota(jnp.int32, (c, 1), 0).astype(F32)
    xi_f = jnp.exp((idx + 1.0) * lg_f)
    xi_b = jnp.exp((c - idx) * lg_b)
    zeta_f = jnp.exp((c - 1.0 - idx) * lg_f)
    zeta_b = jnp.exp(idx * lg_b)
    decay_f = jnp.exp(c * lg_f)
    decay_b = jnp.exp(c * lg_b)

    def intra(q, k, v):
        return _dot((_dot_nt(q, k) * dmat).astype(BF16), v)

    def state_updates(k, v):
        k32 = k.astype(F32)
        return _dot_tn((k32 * zeta_f).astype(BF16), v), _dot_tn((k32 * zeta_b).astype(BF16), v)

    def finish(o, gate):
        return (_rms(o) * gate.astype(F32)).astype(BF16)

    qz, kz, vz = qz_ref[...], kz_ref[...], vz_ref[...]
    oz_ref[...] = finish(intra(qz, kz, vz), gz_ref[...])
    s_f, s_b = state_updates(kz, vz)

    for i in range(n_chunks):
        rows = slice(i * c, (i + 1) * c)
        q, k, v = qx_ref[rows, :], kx_ref[rows, :], vx_ref[rows, :]
        acc_ref[rows, :] = intra(q, k, v) + xi_f * _dot(q, s_f.astype(BF16))
        d_f, d_b = state_updates(k, v)
        dsb_ref[i] = d_b
        s_f = decay_f * s_f + d_f

    for i in reversed(range(n_chunks)):
        rows = slice(i * c, (i + 1) * c)
        o = acc_ref[rows, :] + xi_b * _dot(qx_ref[rows, :], s_b.astype(BF16))
        ox_ref[rows, :] = finish(o, gx_ref[rows, :])
        s_b = decay_b * s_b + dsb_ref[i]


def _ret_call(q, k, v, sg, decay, n_batch, lat_len, ctx_len):
    dq = q.shape[1]
    dv = v.shape[1]
    dkh = dq // RET_HEADS
    dvh = dv // RET_HEADS
    ctx_blk0 = n_batch * lat_len // ctx_len

    def lat(b, h):
        return (b, h)

    def ctx(b, h):
        return (ctx_blk0 + b, h)

    return pl.pallas_call(
        _ret_kernel,
        grid=(n_batch, RET_HEADS),
        in_specs=[
            pl.BlockSpec(decay.shape, lambda b, h: (0, 0)),
            pl.BlockSpec((lat_len, dkh), lat), pl.BlockSpec((ctx_len, dkh), ctx),
            pl.BlockSpec((lat_len, dkh), lat), pl.BlockSpec((ctx_len, dkh), ctx),
            pl.BlockSpec((lat_len, dvh), lat), pl.BlockSpec((ctx_len, dvh), ctx),
            pl.BlockSpec((lat_len, dvh), lat), pl.BlockSpec((ctx_len, dvh), ctx),
        ],
        out_specs=[pl.BlockSpec((lat_len, dvh), lat), pl.BlockSpec((ctx_len, dvh), lambda b, h: (b, h))],
        out_shape=[jax.ShapeDtypeStruct((n_batch * lat_len, dv), BF16),
                   jax.ShapeDtypeStruct((n_batch * ctx_len, dv), BF16)],
        scratch_shapes=[pltpu.VMEM((lat_len, dvh), F32),
                        pltpu.VMEM((lat_len // RET_CHUNK, dkh, dvh), F32)],
        compiler_params=_cparams(2),
        name="retention",
    )(decay, q, q, k, k, v, v, sg, sg)


def _na_tile_plan(n_rows, rows_per_tile):
    kh = min(NA_KH, n_rows)
    span = kh + rows_per_tile
    classes, tile_class = [], []
    for t in range(n_rows // rows_per_tile):
        r0 = t * rows_per_tile
        start = min(max(r0 - kh // 2, 0), n_rows - span)
        sig = []
        for r in range(r0, r0 + rows_per_tile):
            rs = min(max(r - kh // 2, 0), n_rows - kh)
            assert start <= rs and rs + kh <= start + span
            sig.append((rs - start, start - r + NA_KH - 1))
        sig = tuple(sig)
        if sig not in classes:
            classes.append(sig)
        tile_class.append(classes.index(sig))
    return tile_class, classes


def _na_kernel(rpb_ref, q0_ref, *refs, n_rows):
    per_step = ATTN_TILES_PER_STEP
    q_refs, refs = refs[:per_step], refs[per_step:]
    (qz_ref, kx_ref, kz_ref, kxb_ref, kzb_ref, vx_ref, vz_ref,
     ox_ref, oz_ref, tdup_ref, bias_ref, p0_ref, p1_ref, vxa_ref, vza_ref) = refs
    qa_ref = q_refs[0]
    pair = pl.program_id(0)
    b = pl.program_id(1)
    u = pl.program_id(2)
    w = GRID_W
    tq = qa_ref.shape[0]
    rows_per_tile = tq // w
    n_lat_tiles = n_rows // rows_per_tile
    kh = min(NA_KH, n_rows)
    span = kh + rows_per_tile
    n_win = span * w
    tile_class, classes = _na_tile_plan(n_rows, rows_per_tile)
    n_dr = 2 * NA_KH - 1
    n_dc = 2 * NA_KW - 1
    neg_inf = jnp.full((w, LANES), -jnp.inf, F32)

    @pl.when(jnp.logical_and(b == 0, u == 0))
    def _():
        col_q = lax.broadcasted_iota(jnp.int32, (w, LANES), 0)
        lane = lax.broadcasted_iota(jnp.int32, (w, LANES), 1)
        col_k = lane % w
        dc = col_k - col_q + (NA_KW - 1)
        cs = jnp.clip(col_q - NA_KW // 2, 0, w - NA_KW)
        in_cols = jnp.logical_and(col_k >= cs, col_k < cs + NA_KW)
        for hh in range(2):
            for dr in range(n_dr):
                base = ((pair * 2 + hh) * n_dr + dr) * n_dc
                toe = lax.fori_loop(0, n_dc, lambda i, acc: jnp.where(dc == i, rpb_ref[base + i], acc), neg_inf)
                tdup_ref[hh, dr] = jnp.where(in_cols, toe * LOG2E, -jnp.inf)
        for hh in range(2):
            for ci, sig in enumerate(classes):
                for rr, (first, dr0) in enumerate(sig):
                    for j in range(n_win // LANES):
                        halves = []
                        for a in (2 * j, 2 * j + 1):
                            ok = first <= a < first + kh
                            halves.append(tdup_ref[hh, dr0 + a] if ok else neg_inf)
                        bias_ref[hh, ci, rr * w:(rr + 1) * w, j * LANES:(j + 1) * LANES] = (
                            jnp.where(lane < w, halves[0], halves[1]))

    def window_start(t):
        if isinstance(t, int):
            return min(max(t * rows_per_tile - kh // 2, 0), n_rows - span) * w
        start_row = jnp.clip(t * rows_per_tile - kh // 2, 0, n_rows - span)
        return pl.multiple_of(start_row * w, w)

    def tile_class_of(t):
        if isinstance(t, int):
            return tile_class[t]
        cls = jnp.int32(tile_class[0])
        for tt in range(1, n_lat_tiles):
            cls = jnp.where(t == tt, tile_class[tt], cls)
        return cls

    def qk_softmax(q_ref, kx, kz, t, p_ref):
        kwin = kx[pl.ds(window_start(t), n_win), :]
        kz = kz[...]
        cls = tile_class_of(t)
        for hh, qm in enumerate(_lane_halves(q_ref[...])):
            s_lat = _dot_nt(qm, kwin) + bias_ref[hh, cls]
            s_ctx = _dot_nt(qm, kz)
            m = _row_max([s_lat, s_ctx])
            p_ref[hh, :, :n_win] = jnp.exp2(s_lat - m).astype(BF16)
            p_ref[hh, :, n_win:] = jnp.exp2(s_ctx - m).astype(BF16)

    def pv(t, p_ref):
        vwin = vxa_ref[pl.ds(window_start(t), n_win), :]
        vz = vza_ref[...]
        outs = [_normalised(_dot(p_ref[hh, :, :n_win], vwin) + _dot(p_ref[hh, :, n_win:], vz)) for hh in range(2)]
        return _merge_halves(*outs).astype(BF16)

    @pl.when(jnp.logical_and(b == 0, u == 0))
    def _():
        qk_softmax(q0_ref, kx_ref, kz_ref, 0, p0_ref)

    @pl.when(u == 0)
    def _():
        vxa_ref[...] = _with_ones_column(vx_ref[...])
        vza_ref[...] = _with_ones_column(vz_ref[...])
        kz = kz_ref[...]
        vz = vza_ref[...]
        outs = [_softmax_pv([_dot_nt(qm, kz)], [vz]) for qm in _lane_halves(qz_ref[...])]
        oz_ref[...] = _merge_halves(*outs).astype(BF16)

    slots = (p0_ref, p1_ref)
    for j in range(per_step):
        t = per_step * u + j
        if j == per_step - 1:
            t_next = jnp.where(u == n_lat_tiles // per_step - 1, 0, t + 1)
            qk_softmax(q_refs[j], kxb_ref, kzb_ref, t_next, slots[(j + 1) % 2])
        else:
            qk_softmax(q_refs[j], kx_ref, kz_ref, t + 1, slots[(j + 1) % 2])
        ox_ref[j * tq:(j + 1) * tq, :] = pv(t, slots[j % 2])


def _na_call(q, k, v, rpb, n_batch, lat_len, ctx_len):
    d = q.shape[1]
    tq = ctx_len
    assert tq % GRID_W == 0
    n_rows = lat_len // GRID_W
    rows_per_tile = tq // GRID_W
    n_win = (min(NA_KH, n_rows) + rows_per_tile) * GRID_W
    n_classes = len(_na_tile_plan(n_rows, rows_per_tile)[1])
    q_specs, kv_specs, out_specs = _attn_specs(NA_HEADS // 2, n_batch, lat_len, ctx_len, tq, True, False)
    return pl.pallas_call(
        functools.partial(_na_kernel, n_rows=n_rows),
        grid=(NA_HEADS // 2, n_batch, lat_len // (ATTN_TILES_PER_STEP * tq)),
        in_specs=[pl.BlockSpec(memory_space=pltpu.SMEM)] + q_specs + kv_specs,
        out_specs=out_specs,
        out_shape=[jax.ShapeDtypeStruct((n_batch * lat_len, d), BF16),
                   jax.ShapeDtypeStruct((n_batch * ctx_len, d), BF16)],
        scratch_shapes=[pltpu.VMEM((2, 2 * NA_KH - 1, GRID_W, LANES), F32),
                        pltpu.VMEM((2, n_classes, tq, n_win), F32),
                        pltpu.VMEM((2, tq, n_win + ctx_len), BF16),
                        pltpu.VMEM((2, tq, n_win + ctx_len), BF16),
                        pltpu.VMEM((lat_len, 2 * LANES), BF16),
                        pltpu.VMEM((ctx_len, 2 * LANES), BF16)],
        compiler_params=_cparams(3, ordered=True),
        name="nbr_attn",
    )(rpb.astype(F32).reshape(-1), *([q] * len(q_specs)), k, k, k, k, v, v)


def _rope_tables(lat_len, head_dim, tm):
    f = head_dim // 4
    t = jnp.arange(lat_len)
    inv = ROPE_BASE ** (-jnp.arange(f, dtype=F32) / f)
    ang = jnp.concatenate([(t // GRID_W).astype(F32)[:, None] * inv, (t % GRID_W).astype(F32)[:, None] * inv], axis=-1)
    cos, sin = jnp.cos(ang), jnp.sin(ang)
    if head_dim // 2 < LANES:
        reps = LANES // head_dim
        cos = jnp.tile(jnp.concatenate([cos, cos], axis=-1), (1, reps))
        sin = jnp.tile(jnp.concatenate([-sin, sin], axis=-1), (1, reps))
    cos = jnp.concatenate([cos, jnp.ones((tm, LANES), F32)], axis=0)
    sin = jnp.concatenate([sin, jnp.zeros((tm, LANES), F32)], axis=0)
    return cos, sin


def kernel(x, c, ctx, c_ctx, w_ada, b_ada, norm_g, ffn_in, ffn_out, final_g, da_w_qkv, da_w_o, da_lambda, da_subln,
           ret_w_in, ret_w_o, ret_decay, na_w_qkv, na_w_o, na_rpb):
    n_batch, lat_len, d = x.shape
    ctx_len = ctx.shape[1]
    assert n_batch + 1 <= MOD_ROWS
    rows = _Rows(n_batch * lat_len, lat_len, n_batch * ctx_len, tm=512)

    cond = jnp.concatenate([c, c_ctx[None, :], jnp.zeros((MOD_ROWS - n_batch - 1, d), F32)], axis=0)
    mods = _ada_call(cond, w_ada, b_ada).reshape(DEPTH, MOD_ROWS, 6, d)
    norm_g2 = norm_g.reshape(2 * DEPTH, 1, d)
    streams = (x.reshape(rows.n_lat, d), ctx.reshape(rows.n_ctx, d))

    rope64 = _rope_tables(lat_len, d // (2 * DA_HEADS), rows.tm)
    rope256 = _rope_tables(lat_len, d // RET_HEADS, rows.tm)

    ia = ib = ic = 0
    for li in range(DEPTH):
        kind = li % 3
        final = li == DEPTH - 1
        if kind == 0:
            q, k, v = _pre_call(_pre_da_kernel, streams, mods, norm_g2, li, da_w_qkv[ia].astype(BF16), rope64,
                                (d, d, d), rows, "pre_diff_attn")
            lam_init = 0.8 - 0.6 * math.exp(-0.3 * li)
            a_parts = _da_call(q, k, v, da_lambda[ia], da_subln[ia], lam_init, n_batch, lat_len, ctx_len, not final)
            w_o = da_w_o[ia]
            ia += 1
        elif kind == 1:
            q, k, v, sg = _pre_call(_pre_ret_kernel, streams, mods, norm_g2, li, ret_w_in[ib].astype(BF16), rope256,
                                    (d, d, 2 * d, 2 * d), rows, "pre_retention")
            a_parts = _ret_call(q, k, v, sg, ret_decay[ib], n_batch, lat_len, ctx_len)
            w_o = ret_w_o[ib]
            ib += 1
        else:
            q, k, v = _pre_call(_pre_na_kernel, streams, mods, norm_g2, li, na_w_qkv[ic].astype(BF16), None,
                                (d, d, d), rows, "pre_nbr_attn")
            a_parts = _na_call(q, k, v, na_rpb[ic], n_batch, lat_len, ctx_len)
            w_o = na_w_o[ic]
            ic += 1
        streams = (_post_call(a_parts, streams, mods, norm_g2, li, w_o.astype(BF16), ffn_in[li].astype(BF16),
                              ffn_out[li].astype(BF16), final_g, rows, final),)
    return streams[0].reshape(n_batch, lat_len, d)
```

```python
import functools
import math

import jax
import jax.numpy as jnp
from jax import lax
from jax.experimental import pallas as pl
from jax.experimental.pallas import tpu as pltpu

F32 = jnp.float32
BF16 = jnp.bfloat16

DEPTH = 4
GRID_W = 64
DA_HEADS = 8
RET_HEADS = 4
NA_HEADS = 16
NA_KH = 8
NA_KW = 16
ROPE_BASE = 10000.0
EPS = 1e-6
LOG2E = 1.4426950408889634
QK_SCALE = 0.125 * LOG2E
LANES = 128
RET_CHUNK = 256
ATTN_TILES_PER_STEP = 8
MOD_ROWS = 16
VMEM_LIMIT = 56 * 1024 * 1024


def _cparams(n_axes, ordered=False):
    sem = ("arbitrary" if ordered else "parallel",) * n_axes
    return pltpu.CompilerParams(dimension_semantics=sem, vmem_limit_bytes=VMEM_LIMIT)


def _resident(stacked, layer):
    return pl.BlockSpec((None,) + stacked.shape[1:], lambda *_: (layer, 0, 0), pipeline_mode=pl.Buffered(1))


def _rms(x):
    return x * lax.rsqrt(jnp.mean(x * x, axis=-1, keepdims=True) + EPS)


def _modulate(x, g, shift, scale):
    return _rms(x) * g * (1.0 + scale) + shift


def _silu(x):
    return x * jax.nn.sigmoid(x)


def _dot(a, b):
    return jnp.dot(a, b, preferred_element_type=F32)


def _dot_nt(a, b):
    return lax.dot_general(a, b, (((1,), (1,)), ((), ())), preferred_element_type=F32)


def _dot_tn(a, b):
    return lax.dot_general(a, b, (((0,), (0,)), ((), ())), preferred_element_type=F32)


class _Rows:
    def __init__(self, n_lat, lat_len, n_ctx, tm):
        assert lat_len % tm == 0 and n_ctx % tm == 0
        self.n_lat, self.n_ctx, self.tm = n_lat, n_ctx, tm
        self.lat_tiles = n_lat // tm
        self.ctx_tiles = n_ctx // tm
        self.tiles_per_batch = lat_len // tm
        self.n_batch = n_lat // lat_len

    def is_lat(self, i):
        return i < self.lat_tiles

    def mod_row(self, i):
        return jnp.where(i < self.lat_tiles, i // self.tiles_per_batch, self.n_batch)

    def lat_block(self, i):
        return jnp.minimum(i, self.lat_tiles - 1)

    def ctx_block(self, i):
        return jnp.maximum(i - self.lat_tiles, 0)

    def split_specs(self, width):
        return [pl.BlockSpec((self.tm, width), lambda i: (self.lat_block(i), 0)),
                pl.BlockSpec((self.tm, width), lambda i: (self.ctx_block(i), 0))]


def _pick(i, rows, lat_ref, ctx_ref):
    return jnp.where(rows.is_lat(i), lat_ref[...], ctx_ref[...])


def _ada_kernel(c_ref, w_ref, b_ref, o_ref):
    a = _silu(c_ref[...]).astype(BF16)
    o_ref[...] = _dot(a, w_ref[...].astype(BF16)) + b_ref[...]


def _ada_call(cond, w_ada, b_ada):
    depth, d, d6 = w_ada.shape
    tn = 1536
    return pl.pallas_call(
        _ada_kernel,
        grid=(depth, d6 // tn),
        in_specs=[
            pl.BlockSpec((MOD_ROWS, d), lambda l, j: (0, 0)),
            pl.BlockSpec((None, d, tn), lambda l, j: (l, 0, j)),
            pl.BlockSpec((None, 1, tn), lambda l, j: (l, 0, j)),
        ],
        out_specs=pl.BlockSpec((None, MOD_ROWS, tn), lambda l, j: (l, 0, j)),
        out_shape=jax.ShapeDtypeStruct((depth, MOD_ROWS, d6), F32),
        compiler_params=_cparams(2),
        name="ada_mod",
    )(cond, w_ada, b_ada.reshape(depth, 1, d6))


def _rope64_store(y, cos, sin, first_half, mul, dst):
    for j in range(y.shape[1] // LANES):
        yj = y[:, j * LANES:(j + 1) * LANES]
        sw = jnp.where(first_half, pltpu.roll(yj, LANES - 32, 1), pltpu.roll(yj, 32, 1))
        dst[:, j * LANES:(j + 1) * LANES] = ((yj * cos + sw * sin) * mul).astype(BF16)


def _pre_hidden(rows, s_refs, mod_ref, g_ref):
    x = s_refs[0][...] if len(s_refs) == 1 else _pick(pl.program_id(0), rows, *s_refs)
    return _modulate(x, g_ref[...], mod_ref[0:1, :], mod_ref[1:2, :]).astype(BF16)


def _pre_da_kernel(*refs, rows, n_streams):
    s_refs, (mod_ref, g_ref, w_ref, cos_ref, sin_ref, q_ref, k_ref, v_ref) = refs[:n_streams], refs[n_streams:]
    d = w_ref.shape[0]
    h = _pre_hidden(rows, s_refs, mod_ref, g_ref)
    cos = cos_ref[...]
    sin = sin_ref[...]
    first_half = (lax.broadcasted_iota(jnp.int32, cos.shape, 1) % 64) < 32
    _rope64_store(_dot(h, w_ref[:, 0:d]), cos, sin, first_half, QK_SCALE, q_ref)
    _rope64_store(_dot(h, w_ref[:, d:2 * d]), cos, sin, first_half, 1.0, k_ref)
    v_ref[...] = _dot(h, w_ref[:, 2 * d:3 * d]).astype(BF16)


def _pre_na_kernel(*refs, rows, n_streams):
    s_refs, (mod_ref, g_ref, w_ref, q_ref, k_ref, v_ref) = refs[:n_streams], refs[n_streams:]
    d = w_ref.shape[0]
    h = _pre_hidden(rows, s_refs, mod_ref, g_ref)
    q_ref[...] = (_dot(h, w_ref[:, 0:d]) * QK_SCALE).astype(BF16)
    k_ref[...] = _dot(h, w_ref[:, d:2 * d]).astype(BF16)
    v_ref[...] = _dot(h, w_ref[:, 2 * d:3 * d]).astype(BF16)


def _pre_ret_kernel(*refs, rows, n_streams):
    s_refs, (mod_ref, g_ref, w_ref, cos_ref, sin_ref, q_ref, k_ref, v_ref, sg_ref) = refs[:n_streams], refs[n_streams:]
    d = w_ref.shape[0]
    h = _pre_hidden(rows, s_refs, mod_ref, g_ref)
    cos = cos_ref[...]
    sin = sin_ref[...]
    dk = d // RET_HEADS
    for dst, col0, mul in ((q_ref, 0, 1.0), (k_ref, d, dk ** -0.5)):
        y = _dot(h, w_ref[:, col0:col0 + d])
        for hh in range(RET_HEADS):
            a0 = hh * dk
            x1 = y[:, a0:a0 + LANES]
            x2 = y[:, a0 + LANES:a0 + 2 * LANES]
            dst[:, a0:a0 + LANES] = ((x1 * cos - x2 * sin) * mul).astype(BF16)
            dst[:, a0 + LANES:a0 + 2 * LANES] = ((x2 * cos + x1 * sin) * mul).astype(BF16)
    v_ref[...] = _dot(h, w_ref[:, 2 * d:4 * d]).astype(BF16)
    sg_ref[...] = _silu(_dot(h, w_ref[:, 4 * d:6 * d])).astype(BF16)


def _pre_call(kernel_fn, streams, mods, norm_g2, li, w, w_layer, rope, out_widths, rows, name):
    d = w.shape[1]
    tm = rows.tm
    n = rows.n_lat + rows.n_ctx

    def rope_map(i):
        return (jnp.where(rows.is_lat(i), i % rows.tiles_per_batch, rows.tiles_per_batch), 0)

    in_specs = [pl.BlockSpec((tm, d), lambda i: (i, 0))] if len(streams) == 1 else rows.split_specs(d)
    in_specs += [
        pl.BlockSpec((None, None, 6, d), lambda i: (li, rows.mod_row(i), 0, 0)),
        pl.BlockSpec((None, 1, d), lambda i: (2 * li, 0, 0)),
        _resident(w, w_layer),
    ]
    args = list(streams) + [mods, norm_g2, w]
    if rope is not None:
        in_specs += [pl.BlockSpec((tm, LANES), rope_map)] * 2
        args += list(rope)
    return pl.pallas_call(
        functools.partial(kernel_fn, rows=rows, n_streams=len(streams)),
        grid=(n // tm,),
        in_specs=in_specs,
        out_specs=[pl.BlockSpec((tm, wd), lambda i: (i, 0)) for wd in out_widths],
        out_shape=[jax.ShapeDtypeStruct((n, wd), BF16) for wd in out_widths],
        compiler_params=_cparams(1),
        name=name,
    )(*args)


def _post_kernel(*refs, rows, n_a, n_streams, final):
    a_refs, s_refs = refs[:n_a], refs[n_a:n_a + n_streams]
    mod_ref, g_ref, wo_ref, win_ref, wout_ref = refs[n_a + n_streams:n_a + n_streams + 5]
    o_ref = refs[-1]
    i = pl.program_id(0)
    fh = wout_ref.shape[0]
    a = a_refs[0][...] if n_a == 1 else _pick(i, rows, *a_refs)
    s = s_refs[0][...] if n_streams == 1 else _pick(i, rows, *s_refs)
    x = s + mod_ref[2:3, :] * _dot(a, wo_ref[...])
    h = _modulate(x, g_ref[...], mod_ref[3:4, :], mod_ref[4:5, :]).astype(BF16)
    hm = _dot(h, win_ref[...])
    act = (_silu(hm[:, :fh]) * hm[:, fh:]).astype(BF16)
    x = x + mod_ref[5:6, :] * _dot(act, wout_ref[...])
    if final:
        o_ref[...] = _rms(x) * refs[-2][...]
    else:
        o_ref[...] = x


def _post_call(a_parts, streams, mods, norm_g2, li, w_o, w_o_layer, w_in, w_out, final_g, rows, final):
    d = w_o.shape[2]
    tm = rows.tm
    n_rows = rows.n_lat if final else rows.n_lat + rows.n_ctx
    ka = a_parts[0].shape[1]
    in_specs = [pl.BlockSpec((tm, ka), lambda i: (i, 0))] if len(a_parts) == 1 else rows.split_specs(ka)
    in_specs += [pl.BlockSpec((tm, d), lambda i: (i, 0))] if len(streams) == 1 else rows.split_specs(d)
    in_specs += [
        pl.BlockSpec((None, None, 6, d), lambda i: (li, rows.mod_row(i), 0, 0)),
        pl.BlockSpec((None, 1, d), lambda i: (2 * li + 1, 0, 0)),
        _resident(w_o, w_o_layer),
        _resident(w_in, li),
        _resident(w_out, li),
    ]
    args = list(a_parts) + list(streams) + [mods, norm_g2, w_o, w_in, w_out]
    if final:
        in_specs.append(pl.BlockSpec((1, d), lambda i: (0, 0)))
        args.append(final_g.reshape(1, d))
    return pl.pallas_call(
        functools.partial(_post_kernel, rows=rows, n_a=len(a_parts), n_streams=len(streams), final=final),
        grid=(n_rows // tm,),
        in_specs=in_specs,
        out_specs=pl.BlockSpec((tm, d), lambda i: (i, 0)),
        out_shape=jax.ShapeDtypeStruct((n_rows, d), F32),
        compiler_params=_cparams(1),
        name="post_final" if final else "post",
    )(*args)


def _with_ones_column(v):
    extra = (lax.broadcasted_iota(jnp.int32, v.shape, 1) == 0).astype(F32).astype(BF16)
    return jnp.concatenate([v, extra], axis=1)


def _row_max(scores):
    m = scores[0].max(axis=-1, keepdims=True)
    for s in scores[1:]:
        m = jnp.maximum(m, s.max(axis=-1, keepdims=True))
    return m


def _normalised(r):
    return r[:, :LANES] / r[:, LANES:LANES + 1]


def _softmax_pv(scores, values):
    m = _row_max(scores)
    r = None
    for s, v in zip(scores, values):
        part = _dot(jnp.exp2(s - m).astype(BF16), v)
        r = part if r is None else r + part
    return _normalised(r)


def _lane_halves(q):
    q = q.astype(F32)
    lane = lax.broadcasted_iota(jnp.int32, q.shape, 1)
    return jnp.where(lane < 64, q, 0.0).astype(BF16), jnp.where(lane >= 64, q, 0.0).astype(BF16)


def _merge_halves(lo, hi):
    lane = lax.broadcasted_iota(jnp.int32, lo.shape, 1)
    return jnp.where(lane < 64, lo, hi)


def _at_sequence_start(u, n_steps, fn):
    if n_steps == 1:
        fn()
    else:
        pl.when(u == 0)(fn)


def _da_kernel(*refs, lam_init, need_ctx):
    lam_ref, sub_ref, q0_ref = refs[:3]
    q_refs, refs = refs[3:3 + ATTN_TILES_PER_STEP], refs[3 + ATTN_TILES_PER_STEP:]
    qa_ref = q_refs[0]
    if need_ctx:
        qz_ref, refs = refs[0], refs[1:]
    kx_ref, kz_ref, kxb_ref, kzb_ref, vx_ref, vz_ref, ox_ref = refs[:7]
    refs = refs[7:]
    if need_ctx:
        oz_ref, refs = refs[0], refs[1:]
    p0_ref, p1_ref, va_ref = refs

    u = pl.program_id(2)
    tq = qa_ref.shape[0]
    n_lat = kx_ref.shape[0]
    lp = lam_ref[...]
    lam = (jnp.exp(jnp.sum(lp[0:1, :] * lp[1:2, :], axis=-1, keepdims=True))
           - jnp.exp(jnp.sum(lp[2:3, :] * lp[3:4, :], axis=-1, keepdims=True)) + lam_init)

    def qk_softmax(q_ref, kx, kz, p_ref):
        qs = jnp.concatenate(_lane_halves(q_ref[...]), axis=0)
        sx = _dot_nt(qs, kx[...])
        sz = _dot_nt(qs, kz[...])
        m = _row_max([sx, sz])
        p_ref[:, :n_lat] = jnp.exp2(sx - m).astype(BF16)
        p_ref[:, n_lat:] = jnp.exp2(sz - m).astype(BF16)

    def finish(maps):
        acc = maps[0] - lam * maps[1]
        return (_rms(acc) * sub_ref[...] * (1.0 - lam_init)).astype(BF16)

    def pv(p_ref):
        r = _normalised(_dot(p_ref[...], va_ref[...]))
        return finish([r[:tq], r[tq:]])

    @pl.when(jnp.logical_and(jnp.logical_and(pl.program_id(0) == 0, pl.program_id(1) == 0), u == 0))
    def _():
        qk_softmax(q0_ref, kx_ref, kz_ref, p0_ref)

    def sequence_start():
        va_ref[:n_lat, :] = _with_ones_column(vx_ref[...])
        va_ref[n_lat:, :] = _with_ones_column(vz_ref[...])
        if need_ctx:
            kz = kz_ref[...]
            vz = va_ref[n_lat:, :]
            oz_ref[...] = finish([_softmax_pv([_dot_nt(qm, kz)], [vz]) for qm in _lane_halves(qz_ref[...])])

    _at_sequence_start(u, n_lat // ox_ref.shape[0], sequence_start)

    slots = (p0_ref, p1_ref)
    for j in range(ATTN_TILES_PER_STEP):
        keys = (kxb_ref, kzb_ref) if j == ATTN_TILES_PER_STEP - 1 else (kx_ref, kz_ref)
        qk_softmax(q_refs[j], *keys, slots[(j + 1) % 2])
        ox_ref[j * tq:(j + 1) * tq, :] = pv(slots[j % 2])


def _attn_specs(n_head_blocks, n_batch, lat_len, ctx_len, tq, need_ctx, chain_heads):
    n_lat_tiles = lat_len // tq
    per_step = ATTN_TILES_PER_STEP
    assert per_step % 2 == 0 and n_lat_tiles % per_step == 0
    n_steps = n_lat_tiles // per_step
    ctx_blk0 = n_batch * lat_len // ctx_len

    def ahead(h, b, u):
        if chain_heads:
            s = jnp.minimum(h * n_batch + b + 1, n_head_blocks * n_batch - 1)
            hn, bn = s // n_batch, s % n_batch
        else:
            hn, bn = h, jnp.minimum(b + 1, n_batch - 1)
        last = u == n_steps - 1
        return jnp.where(last, hn, h), jnp.where(last, bn, b), jnp.where(last, 0, per_step * (u + 1))

    def ahead_map(h, b, u):
        hn, bn, tn = ahead(h, b, u)
        return (bn * n_lat_tiles + tn, hn)

    def tile_map(j):
        return lambda h, b, u: (b * n_lat_tiles + per_step * u + j, h)

    q_specs = [pl.BlockSpec((tq, LANES), lambda h, b, u: (0, h))]
    q_specs += [pl.BlockSpec((tq, LANES), tile_map(j)) for j in range(1, per_step)]
    q_specs.append(pl.BlockSpec((tq, LANES), ahead_map))
    if need_ctx:
        q_specs.append(pl.BlockSpec((ctx_len, LANES), lambda h, b, u: (ctx_blk0 + b, h)))
    lat_spec = pl.BlockSpec((lat_len, LANES), lambda h, b, u: (b, h))
    ctx_spec = pl.BlockSpec((ctx_len, LANES), lambda h, b, u: (ctx_blk0 + b, h))
    kv_specs = [
        lat_spec, ctx_spec,
        pl.BlockSpec((lat_len, LANES), lambda h, b, u: (ahead(h, b, u)[1], ahead(h, b, u)[0])),
        pl.BlockSpec((ctx_len, LANES), lambda h, b, u: (ctx_blk0 + ahead(h, b, u)[1], ahead(h, b, u)[0])),
        lat_spec, ctx_spec,
    ]
    out_specs = [pl.BlockSpec((per_step * tq, LANES), lambda h, b, u: (b * n_steps + u, h))]
    if need_ctx:
        out_specs.append(pl.BlockSpec((ctx_len, LANES), lambda h, b, u: (b, h)))
    return q_specs, kv_specs, out_specs


def _da_call(q, k, v, lam_p, subln, lam_init, n_batch, lat_len, ctx_len, need_ctx):
    d = q.shape[1]
    tq = ctx_len
    q_specs, kv_specs, out_specs = _attn_specs(DA_HEADS, n_batch, lat_len, ctx_len, tq, need_ctx, True)
    out_shape = [jax.ShapeDtypeStruct((n_batch * lat_len, d), BF16)]
    if need_ctx:
        out_shape.append(jax.ShapeDtypeStruct((n_batch * ctx_len, d), BF16))
    n_q = len(q_specs)
    return pl.pallas_call(
        functools.partial(_da_kernel, lam_init=lam_init, need_ctx=need_ctx),
        grid=(DA_HEADS, n_batch, lat_len // (ATTN_TILES_PER_STEP * tq)),
        in_specs=[pl.BlockSpec(lam_p.shape, lambda h, b, u: (0, 0)),
                  pl.BlockSpec((1, LANES), lambda h, b, u: (0, 0))] + q_specs + kv_specs,
        out_specs=out_specs,
        out_shape=out_shape,
        scratch_shapes=[pltpu.VMEM((2 * tq, lat_len + ctx_len), BF16),
                        pltpu.VMEM((2 * tq, lat_len + ctx_len), BF16),
                        pltpu.VMEM((lat_len + ctx_len, 2 * LANES), BF16)],
        compiler_params=_cparams(3, ordered=True),
        name="diff_attn",
    )(lam_p, subln.reshape(1, LANES), *([q] * n_q), k, k, k, k, v, v)


def _ret_kernel(dec_ref, qx_ref, qz_ref, kx_ref, kz_ref, vx_ref, vz_ref, gx_ref, gz_ref, ox_ref, oz_ref,
                acc_ref, dsb_ref):
    c = RET_CHUNK
    n_chunks = qx_ref.shape[0] // c
    head = pl.program_id(1)
    lgs = jax.nn.log_sigmoid(dec_ref[...])
    sel = lax.broadcasted_iota(jnp.int32, lgs.shape, 1) == head
    lgs = jnp.sum(jnp.where(sel, lgs, 0.0), axis=-1, keepdims=True)
    lg_f = lgs[0:1, :]
    lg_b = lgs[1:2, :]

    ii = lax.broadcasted_iota(jnp.int32, (c, c), 0)
    jj = lax.broadcasted_iota(jnp.int32, (c, c), 1)
    diff = (ii - jj).astype(F32)
    dmat = jnp.where(diff >= 0.0, jnp.exp(jnp.maximum(diff, 0.0) * lg_f), jnp.exp(jnp.maximum(-diff, 0.0) * lg_b))
    idx = lax.broadcasted_iota(jnp.int32, (c, 1), 0).astype(F32)
    xi_f = jnp.exp((idx + 1.0) * lg_f)
    xi_b = jnp.exp((c - idx) * lg_b)
    zeta_f = jnp.exp((c - 1.0 - idx) * lg_f)
    zeta_b = jnp.exp(idx * lg_b)
    decay_f = jnp.exp(c * lg_f)
    decay_b = jnp.exp(c * lg_b)

    def intra(q, k, v):
        return _dot((_dot_nt(q, k) * dmat).astype(BF16), v)

    def state_updates(k, v):
        k32 = k.astype(F32)
        return _dot_tn((k32 * zeta_f).astype(BF16), v), _dot_tn((k32 * zeta_b).astype(BF16), v)

    def finish(o, gate):
        return (_rms(o) * gate.astype(F32)).astype(BF16)

    qz, kz, vz = qz_ref[...], kz_ref[...], vz_ref[...]
    oz_ref[...] = finish(intra(qz, kz, vz), gz_ref[...])
    s_f, s_b = state_updates(kz, vz)

    for i in range(n_chunks):
        rows = slice(i * c, (i + 1) * c)
        q, k, v = qx_ref[rows, :], kx_ref[rows, :], vx_ref[rows, :]
        acc_ref[rows, :] = intra(q, k, v) + xi_f * _dot(q, s_f.astype(BF16))
        d_f, d_b = state_updates(k, v)
        dsb_ref[i] = d_b
        s_f = decay_f * s_f + d_f

    for i in reversed(range(n_chunks)):
        rows = slice(i * c, (i + 1) * c)
        o = acc_ref[rows, :] + xi_b * _dot(qx_ref[rows, :], s_b.astype(BF16))
        ox_ref[rows, :] = finish(o, gx_ref[rows, :])
        s_b = decay_b * s_b + dsb_ref[i]


def _ret_call(q, k, v, sg, decay, n_batch, lat_len, ctx_len):
    dq = q.shape[1]
    dv = v.shape[1]
    dkh = dq // RET_HEADS
    dvh = dv // RET_HEADS
    ctx_blk0 = n_batch * lat_len // ctx_len

    def lat(b, h):
        return (b, h)

    def ctx(b, h):
        return (ctx_blk0 + b, h)

    return pl.pallas_call(
        _ret_kernel,
        grid=(n_batch, RET_HEADS),
        in_specs=[
            pl.BlockSpec(decay.shape, lambda b, h: (0, 0)),
            pl.BlockSpec((lat_len, dkh), lat), pl.BlockSpec((ctx_len, dkh), ctx),
            pl.BlockSpec((lat_len, dkh), lat), pl.BlockSpec((ctx_len, dkh), ctx),
            pl.BlockSpec((lat_len, dvh), lat), pl.BlockSpec((ctx_len, dvh), ctx),
            pl.BlockSpec((lat_len, dvh), lat), pl.BlockSpec((ctx_len, dvh), ctx),
        ],
        out_specs=[pl.BlockSpec((lat_len, dvh), lat), pl.BlockSpec((ctx_len, dvh), lambda b, h: (b, h))],
        out_shape=[jax.ShapeDtypeStruct((n_batch * lat_len, dv), BF16),
                   jax.ShapeDtypeStruct((n_batch * ctx_len, dv), BF16)],
        scratch_shapes=[pltpu.VMEM((lat_len, dvh), F32),
                        pltpu.VMEM((lat_len // RET_CHUNK, dkh, dvh), F32)],
        compiler_params=_cparams(2),
        name="retention",
    )(decay, q, q, k, k, v, v, sg, sg)


def _na_tile_plan(n_rows, rows_per_tile):
    kh = min(NA_KH, n_rows)
    span = kh + rows_per_tile
    classes, tile_class = [], []
    for t in range(n_rows // rows_per_tile):
        r0 = t * rows_per_tile
        start = min(max(r0 - kh // 2, 0), n_rows - span)
        sig = []
        for r in range(r0, r0 + rows_per_tile):
            rs = min(max(r - kh // 2, 0), n_rows - kh)
            assert start <= rs and rs + kh <= start + span
            sig.append((rs - start, start - r + NA_KH - 1))
        sig = tuple(sig)
        if sig not in classes:
            classes.append(sig)
        tile_class.append(classes.index(sig))
    return tile_class, classes


def _na_kernel(rpb_ref, q0_ref, *refs, n_rows):
    per_step = ATTN_TILES_PER_STEP
    q_refs, refs = refs[:per_step], refs[per_step:]
    (qz_ref, kx_ref, kz_ref, kxb_ref, kzb_ref, vx_ref, vz_ref,
     ox_ref, oz_ref, tdup_ref, bias_ref, p0_ref, p1_ref, vxa_ref, vza_ref) = refs
    qa_ref = q_refs[0]
    pair = pl.program_id(0)
    b = pl.program_id(1)
    u = pl.program_id(2)
    w = GRID_W
    tq = qa_ref.shape[0]
    rows_per_tile = tq // w
    n_lat_tiles = n_rows // rows_per_tile
    kh = min(NA_KH, n_rows)
    span = kh + rows_per_tile
    n_win = span * w
    tile_class, classes = _na_tile_plan(n_rows, rows_per_tile)
    n_dr = 2 * NA_KH - 1
    n_dc = 2 * NA_KW - 1
    neg_inf = jnp.full((w, LANES), -jnp.inf, F32)

    @pl.when(jnp.logical_and(b == 0, u == 0))
    def _():
        col_q = lax.broadcasted_iota(jnp.int32, (w, LANES), 0)
        lane = lax.broadcasted_iota(jnp.int32, (w, LANES), 1)
        col_k = lane % w
        dc = col_k - col_q + (NA_KW - 1)
        cs = jnp.clip(col_q - NA_KW // 2, 0, w - NA_KW)
        in_cols = jnp.logical_and(col_k >= cs, col_k < cs + NA_KW)
        for hh in range(2):
            for dr in range(n_dr):
                base = ((pair * 2 + hh) * n_dr + dr) * n_dc
                toe = lax.fori_loop(0, n_dc, lambda i, acc: jnp.where(dc == i, rpb_ref[base + i], acc), neg_inf)
                tdup_ref[hh, dr] = jnp.where(in_cols, toe * LOG2E, -jnp.inf)
        for hh in range(2):
            for ci, sig in enumerate(classes):
                for rr, (first, dr0) in enumerate(sig):
                    for j in range(n_win // LANES):
                        halves = []
                        for a in (2 * j, 2 * j + 1):
                            ok = first <= a < first + kh
                            halves.append(tdup_ref[hh, dr0 + a] if ok else neg_inf)
                        bias_ref[hh, ci, rr * w:(rr + 1) * w, j * LANES:(j + 1) * LANES] = (
                            jnp.where(lane < w, halves[0], halves[1]))

    def window_start(t):
        if isinstance(t, int):
            return min(max(t * rows_per_tile - kh // 2, 0), n_rows - span) * w
        start_row = jnp.clip(t * rows_per_tile - kh // 2, 0, n_rows - span)
        return pl.multiple_of(start_row * w, w)

    def tile_class_of(t):
        if isinstance(t, int):
            return tile_class[t]
        cls = jnp.int32(tile_class[0])
        for tt in range(1, n_lat_tiles):
            cls = jnp.where(t == tt, tile_class[tt], cls)
        return cls

    def qk_softmax(q_ref, kx, kz, t, p_ref):
        kwin = kx[pl.ds(window_start(t), n_win), :]
        kz = kz[...]
        cls = tile_class_of(t)
        for hh, qm in enumerate(_lane_halves(q_ref[...])):
            s_lat = _dot_nt(qm, kwin) + bias_ref[hh, cls]
            s_ctx = _dot_nt(qm, kz)
            m = _row_max([s_lat, s_ctx])
            p_ref[hh, :, :n_win] = jnp.exp2(s_lat - m).astype(BF16)
            p_ref[hh, :, n_win:] = jnp.exp2(s_ctx - m).astype(BF16)

    def pv(t, p_ref):
        vwin = vxa_ref[pl.ds(window_start(t), n_win), :]
        vz = vza_ref[...]
        outs = [_normalised(_dot(p_ref[hh, :, :n_win], vwin) + _dot(p_ref[hh, :, n_win:], vz)) for hh in range(2)]
        return _merge_halves(*outs).astype(BF16)

    @pl.when(jnp.logical_and(b == 0, u == 0))
    def _():
        qk_softmax(q0_ref, kx_ref, kz_ref, 0, p0_ref)

    def sequence_start():
        vxa_ref[...] = _with_ones_column(vx_ref[...])
        vza_ref[...] = _with_ones_column(vz_ref[...])
        kz = kz_ref[...]
        vz = vza_ref[...]
        outs = [_softmax_pv([_dot_nt(qm, kz)], [vz]) for qm in _lane_halves(qz_ref[...])]
        oz_ref[...] = _merge_halves(*outs).astype(BF16)

    _at_sequence_start(u, n_lat_tiles // per_step, sequence_start)

    slots = (p0_ref, p1_ref)
    for j in range(per_step):
        t = per_step * u + j
        if j == per_step - 1:
            t_next = jnp.where(u == n_lat_tiles // per_step - 1, 0, t + 1)
            qk_softmax(q_refs[j], kxb_ref, kzb_ref, t_next, slots[(j + 1) % 2])
        else:
            qk_softmax(q_refs[j], kx_ref, kz_ref, t + 1, slots[(j + 1) % 2])
        ox_ref[j * tq:(j + 1) * tq, :] = pv(t, slots[j % 2])


def _na_call(q, k, v, rpb, n_batch, lat_len, ctx_len):
    d = q.shape[1]
    tq = ctx_len
    assert tq % GRID_W == 0
    n_rows = lat_len // GRID_W
    rows_per_tile = tq // GRID_W
    n_win = (min(NA_KH, n_rows) + rows_per_tile) * GRID_W
    n_classes = len(_na_tile_plan(n_rows, rows_per_tile)[1])
    q_specs, kv_specs, out_specs = _attn_specs(NA_HEADS // 2, n_batch, lat_len, ctx_len, tq, True, False)
    return pl.pallas_call(
        functools.partial(_na_kernel, n_rows=n_rows),
        grid=(NA_HEADS // 2, n_batch, lat_len // (ATTN_TILES_PER_STEP * tq)),
        in_specs=[pl.BlockSpec(memory_space=pltpu.SMEM)] + q_specs + kv_specs,
        out_specs=out_specs,
        out_shape=[jax.ShapeDtypeStruct((n_batch * lat_len, d), BF16),
                   jax.ShapeDtypeStruct((n_batch * ctx_len, d), BF16)],
        scratch_shapes=[pltpu.VMEM((2, 2 * NA_KH - 1, GRID_W, LANES), F32),
                        pltpu.VMEM((2, n_classes, tq, n_win), F32),
                        pltpu.VMEM((2, tq, n_win + ctx_len), BF16),
                        pltpu.VMEM((2, tq, n_win + ctx_len), BF16),
                        pltpu.VMEM((lat_len, 2 * LANES), BF16),
                        pltpu.VMEM((ctx_len, 2 * LANES), BF16)],
        compiler_params=_cparams(3, ordered=True),
        name="nbr_attn",
    )(rpb.astype(F32).reshape(-1), *([q] * len(q_specs)), k, k, k, k, v, v)


def _rope_tables(lat_len, head_dim, tm):
    f = head_dim // 4
    t = jnp.arange(lat_len)
    inv = ROPE_BASE ** (-jnp.arange(f, dtype=F32) / f)
    ang = jnp.concatenate([(t // GRID_W).astype(F32)[:, None] * inv, (t % GRID_W).astype(F32)[:, None] * inv], axis=-1)
    cos, sin = jnp.cos(ang), jnp.sin(ang)
    if head_dim // 2 < LANES:
        reps = LANES // head_dim
        cos = jnp.tile(jnp.concatenate([cos, cos], axis=-1), (1, reps))
        sin = jnp.tile(jnp.concatenate([-sin, sin], axis=-1), (1, reps))
    cos = jnp.concatenate([cos, jnp.ones((tm, LANES), F32)], axis=0)
    sin = jnp.concatenate([sin, jnp.zeros((tm, LANES), F32)], axis=0)
    return cos, sin


def kernel(x, c, ctx, c_ctx, w_ada, b_ada, norm_g, ffn_in, ffn_out, final_g, da_w_qkv, da_w_o, da_lambda, da_subln,
           ret_w_in, ret_w_o, ret_decay, na_w_qkv, na_w_o, na_rpb):
    n_batch, lat_len, d = x.shape
    ctx_len = ctx.shape[1]
    assert n_batch + 1 <= MOD_ROWS
    rows = _Rows(n_batch * lat_len, lat_len, n_batch * ctx_len, tm=512)

    cond = jnp.concatenate([c, c_ctx[None, :], jnp.zeros((MOD_ROWS - n_batch - 1, d), F32)], axis=0)
    mods = _ada_call(cond, w_ada, b_ada).reshape(DEPTH, MOD_ROWS, 6, d)
    norm_g2 = norm_g.reshape(2 * DEPTH, 1, d)
    streams = (x.reshape(rows.n_lat, d), ctx.reshape(rows.n_ctx, d))

    rope64 = _rope_tables(lat_len, d // (2 * DA_HEADS), rows.tm)
    rope256 = _rope_tables(lat_len, d // RET_HEADS, rows.tm)

    da_w_qkv, da_w_o, ret_w_in, ret_w_o, na_w_qkv, na_w_o, ffn_in, ffn_out = (
        w.astype(BF16) for w in (da_w_qkv, da_w_o, ret_w_in, ret_w_o, na_w_qkv, na_w_o, ffn_in, ffn_out))

    ia = ib = ic = 0
    for li in range(DEPTH):
        kind = li % 3
        final = li == DEPTH - 1
        if kind == 0:
            q, k, v = _pre_call(_pre_da_kernel, streams, mods, norm_g2, li, da_w_qkv, ia, rope64,
                                (d, d, d), rows, "pre_diff_attn")
            lam_init = 0.8 - 0.6 * math.exp(-0.3 * li)
            a_parts = _da_call(q, k, v, da_lambda[ia], da_subln[ia], lam_init, n_batch, lat_len, ctx_len, not final)
            w_o, w_o_layer = da_w_o, ia
            ia += 1
        elif kind == 1:
            q, k, v, sg = _pre_call(_pre_ret_kernel, streams, mods, norm_g2, li, ret_w_in, ib, rope256,
                                    (d, d, 2 * d, 2 * d), rows, "pre_retention")
            a_parts = _ret_call(q, k, v, sg, ret_decay[ib], n_batch, lat_len, ctx_len)
            w_o, w_o_layer = ret_w_o, ib
            ib += 1
        else:
            q, k, v = _pre_call(_pre_na_kernel, streams, mods, norm_g2, li, na_w_qkv, ic, None,
                                (d, d, d), rows, "pre_nbr_attn")
            a_parts = _na_call(q, k, v, na_rpb[ic], n_batch, lat_len, ctx_len)
            w_o, w_o_layer = na_w_o, ic
            ic += 1
        streams = (_post_call(a_parts, streams, mods, norm_g2, li, w_o, w_o_layer, ffn_in, ffn_out, final_g, rows,
                              final),)
    return streams[0].reshape(n_batch, lat_len, d)
```

```python
import functools
import math

import jax
import jax.numpy as jnp
from jax import lax
from jax.experimental import pallas as pl
from jax.experimental.pallas import tpu as pltpu

F32 = jnp.float32
BF16 = jnp.bfloat16

DEPTH = 4
GRID_W = 64
DA_HEADS = 8
RET_HEADS = 4
NA_HEADS = 16
NA_KH = 8
NA_KW = 16
ROPE_BASE = 10000.0
EPS = 1e-6
LOG2E = 1.4426950408889634
QK_SCALE = 0.125 * LOG2E
LANES = 128
RET_CHUNK = 256
ATTN_TILES_PER_STEP = 8
MOD_ROWS = 16
VMEM_LIMIT = 56 * 1024 * 1024


def _cparams(n_axes, ordered=False):
    sem = ("arbitrary" if ordered else "parallel",) * n_axes
    return pltpu.CompilerParams(dimension_semantics=sem, vmem_limit_bytes=VMEM_LIMIT)


def _resident(stacked, layer):
    return pl.BlockSpec((None,) + stacked.shape[1:], lambda *_: (layer, 0, 0), pipeline_mode=pl.Buffered(1))


def _rms(x):
    return x * lax.rsqrt(jnp.mean(x * x, axis=-1, keepdims=True) + EPS)


def _modulate(x, g, shift, scale):
    return _rms(x) * g * (1.0 + scale) + shift


def _silu(x):
    return x * jax.nn.sigmoid(x)


def _dot(a, b):
    return jnp.dot(a, b, preferred_element_type=F32)


def _dot_nt(a, b):
    return lax.dot_general(a, b, (((1,), (1,)), ((), ())), preferred_element_type=F32)


def _dot_tn(a, b):
    return lax.dot_general(a, b, (((0,), (0,)), ((), ())), preferred_element_type=F32)


class _Rows:
    def __init__(self, n_lat, lat_len, n_ctx, tm):
        assert lat_len % tm == 0 and n_ctx % tm == 0
        self.n_lat, self.n_ctx, self.tm = n_lat, n_ctx, tm
        self.lat_tiles = n_lat // tm
        self.ctx_tiles = n_ctx // tm
        self.tiles_per_batch = lat_len // tm
        self.n_batch = n_lat // lat_len

    def is_lat(self, i):
        return i < self.lat_tiles

    def mod_row(self, i):
        return jnp.where(i < self.lat_tiles, i // self.tiles_per_batch, self.n_batch)

    def lat_block(self, i):
        return jnp.minimum(i, self.lat_tiles - 1)

    def ctx_block(self, i):
        return jnp.maximum(i - self.lat_tiles, 0)

    def split_specs(self, width):
        return [pl.BlockSpec((self.tm, width), lambda i: (self.lat_block(i), 0)),
                pl.BlockSpec((self.tm, width), lambda i: (self.ctx_block(i), 0))]


def _pick(i, rows, lat_ref, ctx_ref):
    return jnp.where(rows.is_lat(i), lat_ref[...], ctx_ref[...])


def _ada_kernel(c_ref, w_ref, b_ref, o_ref):
    a = _silu(c_ref[...]).astype(BF16)
    o_ref[...] = _dot(a, w_ref[...].astype(BF16)) + b_ref[...]


def _ada_call(cond, w_ada, b_ada):
    depth, d, d6 = w_ada.shape
    tn = 1536
    return pl.pallas_call(
        _ada_kernel,
        grid=(depth, d6 // tn),
        in_specs=[
            pl.BlockSpec((MOD_ROWS, d), lambda l, j: (0, 0)),
            pl.BlockSpec((None, d, tn), lambda l, j: (l, 0, j)),
            pl.BlockSpec((None, 1, tn), lambda l, j: (l, 0, j)),
        ],
        out_specs=pl.BlockSpec((None, MOD_ROWS, tn), lambda l, j: (l, 0, j)),
        out_shape=jax.ShapeDtypeStruct((depth, MOD_ROWS, d6), F32),
        compiler_params=_cparams(2),
        name="ada_mod",
    )(cond, w_ada, b_ada.reshape(depth, 1, d6))


def _rope64_store(y, cos, sin, first_half, mul, dst):
    for j in range(y.shape[1] // LANES):
        yj = y[:, j * LANES:(j + 1) * LANES]
        sw = jnp.where(first_half, pltpu.roll(yj, LANES - 32, 1), pltpu.roll(yj, 32, 1))
        dst[:, j * LANES:(j + 1) * LANES] = ((yj * cos + sw * sin) * mul).astype(BF16)


def _pre_hidden(rows, s_refs, mod_ref, g_ref):
    x = s_refs[0][...] if len(s_refs) == 1 else _pick(pl.program_id(0), rows, *s_refs)
    return _modulate(x, g_ref[...], mod_ref[0:1, :], mod_ref[1:2, :]).astype(BF16)


def _bf16_weights(w_ref, scratch_ref):
    @pl.when(pl.program_id(0) == 0)
    def _():
        scratch_ref[...] = w_ref[...].astype(BF16)
    return scratch_ref


def _pre_da_kernel(*refs, rows, n_streams):
    s_refs, (mod_ref, g_ref, w_ref, cos_ref, sin_ref, q_ref, k_ref, v_ref, wb_ref) = refs[:n_streams], refs[n_streams:]
    w_ref = _bf16_weights(w_ref, wb_ref)
    d = w_ref.shape[0]
    h = _pre_hidden(rows, s_refs, mod_ref, g_ref)
    cos = cos_ref[...]
    sin = sin_ref[...]
    first_half = (lax.broadcasted_iota(jnp.int32, cos.shape, 1) % 64) < 32
    _rope64_store(_dot(h, w_ref[:, 0:d]), cos, sin, first_half, QK_SCALE, q_ref)
    _rope64_store(_dot(h, w_ref[:, d:2 * d]), cos, sin, first_half, 1.0, k_ref)
    v_ref[...] = _dot(h, w_ref[:, 2 * d:3 * d]).astype(BF16)


def _pre_na_kernel(*refs, rows, n_streams):
    s_refs, (mod_ref, g_ref, w_ref, q_ref, k_ref, v_ref, wb_ref) = refs[:n_streams], refs[n_streams:]
    w_ref = _bf16_weights(w_ref, wb_ref)
    d = w_ref.shape[0]
    h = _pre_hidden(rows, s_refs, mod_ref, g_ref)
    q_ref[...] = (_dot(h, w_ref[:, 0:d]) * QK_SCALE).astype(BF16)
    k_ref[...] = _dot(h, w_ref[:, d:2 * d]).astype(BF16)
    v_ref[...] = _dot(h, w_ref[:, 2 * d:3 * d]).astype(BF16)


def _pre_ret_kernel(*refs, rows, n_streams):
    s_refs, (mod_ref, g_ref, w_ref, cos_ref, sin_ref, q_ref, k_ref, v_ref, sg_ref) = refs[:n_streams], refs[n_streams:]
    d = w_ref.shape[0]
    h = _pre_hidden(rows, s_refs, mod_ref, g_ref)
    cos = cos_ref[...]
    sin = sin_ref[...]
    dk = d // RET_HEADS
    for dst, col0, mul in ((q_ref, 0, 1.0), (k_ref, d, dk ** -0.5)):
        y = _dot(h, w_ref[:, col0:col0 + d])
        for hh in range(RET_HEADS):
            a0 = hh * dk
            x1 = y[:, a0:a0 + LANES]
            x2 = y[:, a0 + LANES:a0 + 2 * LANES]
            dst[:, a0:a0 + LANES] = ((x1 * cos - x2 * sin) * mul).astype(BF16)
            dst[:, a0 + LANES:a0 + 2 * LANES] = ((x2 * cos + x1 * sin) * mul).astype(BF16)
    v_ref[...] = _dot(h, w_ref[:, 2 * d:4 * d]).astype(BF16)
    sg_ref[...] = _silu(_dot(h, w_ref[:, 4 * d:6 * d])).astype(BF16)


def _pre_call(kernel_fn, streams, mods, norm_g2, li, w, w_layer, rope, out_widths, rows, name):
    d = w.shape[1]
    cast_in_kernel = w.dtype != BF16
    tm = rows.tm
    n = rows.n_lat + rows.n_ctx

    def rope_map(i):
        return (jnp.where(rows.is_lat(i), i % rows.tiles_per_batch, rows.tiles_per_batch), 0)

    in_specs = [pl.BlockSpec((tm, d), lambda i: (i, 0))] if len(streams) == 1 else rows.split_specs(d)
    in_specs += [
        pl.BlockSpec((None, None, 6, d), lambda i: (li, rows.mod_row(i), 0, 0)),
        pl.BlockSpec((None, 1, d), lambda i: (2 * li, 0, 0)),
        _resident(w, w_layer),
    ]
    args = list(streams) + [mods, norm_g2, w]
    if rope is not None:
        in_specs += [pl.BlockSpec((tm, LANES), rope_map)] * 2
        args += list(rope)
    return pl.pallas_call(
        functools.partial(kernel_fn, rows=rows, n_streams=len(streams)),
        grid=(n // tm,),
        in_specs=in_specs,
        out_specs=[pl.BlockSpec((tm, wd), lambda i: (i, 0)) for wd in out_widths],
        out_shape=[jax.ShapeDtypeStruct((n, wd), BF16) for wd in out_widths],
        scratch_shapes=[pltpu.VMEM(w.shape[1:], BF16)] if cast_in_kernel else [],
        compiler_params=_cparams(1, ordered=cast_in_kernel),
        name=name,
    )(*args)


def _post_kernel(*refs, rows, n_a, n_streams, final):
    a_refs, s_refs = refs[:n_a], refs[n_a:n_a + n_streams]
    mod_ref, g_ref, wo_ref, win_ref, wout_ref = refs[n_a + n_streams:n_a + n_streams + 5]
    o_ref = refs[-1]
    i = pl.program_id(0)
    fh = wout_ref.shape[0]
    a = a_refs[0][...] if n_a == 1 else _pick(i, rows, *a_refs)
    s = s_refs[0][...] if n_streams == 1 else _pick(i, rows, *s_refs)
    x = s + mod_ref[2:3, :] * _dot(a, wo_ref[...])
    h = _modulate(x, g_ref[...], mod_ref[3:4, :], mod_ref[4:5, :]).astype(BF16)
    hm = _dot(h, win_ref[...])
    act = (_silu(hm[:, :fh]) * hm[:, fh:]).astype(BF16)
    x = x + mod_ref[5:6, :] * _dot(act, wout_ref[...])
    if final:
        o_ref[...] = _rms(x) * refs[-2][...]
    else:
        o_ref[...] = x


def _post_call(a_parts, streams, mods, norm_g2, li, w_o, w_o_layer, w_in, w_out, final_g, rows, final):
    d = w_o.shape[2]
    tm = rows.tm
    n_rows = rows.n_lat if final else rows.n_lat + rows.n_ctx
    ka = a_parts[0].shape[1]
    in_specs = [pl.BlockSpec((tm, ka), lambda i: (i, 0))] if len(a_parts) == 1 else rows.split_specs(ka)
    in_specs += [pl.BlockSpec((tm, d), lambda i: (i, 0))] if len(streams) == 1 else rows.split_specs(d)
    in_specs += [
        pl.BlockSpec((None, None, 6, d), lambda i: (li, rows.mod_row(i), 0, 0)),
        pl.BlockSpec((None, 1, d), lambda i: (2 * li + 1, 0, 0)),
        _resident(w_o, w_o_layer),
        _resident(w_in, li),
        _resident(w_out, li),
    ]
    args = list(a_parts) + list(streams) + [mods, norm_g2, w_o, w_in, w_out]
    if final:
        in_specs.append(pl.BlockSpec((1, d), lambda i: (0, 0)))
        args.append(final_g.reshape(1, d))
    return pl.pallas_call(
        functools.partial(_post_kernel, rows=rows, n_a=len(a_parts), n_streams=len(streams), final=final),
        grid=(n_rows // tm,),
        in_specs=in_specs,
        out_specs=pl.BlockSpec((tm, d), lambda i: (i, 0)),
        out_shape=jax.ShapeDtypeStruct((n_rows, d), F32),
        compiler_params=_cparams(1),
        name="post_final" if final else "post",
    )(*args)


def _with_ones_column(v):
    extra = (lax.broadcasted_iota(jnp.int32, v.shape, 1) == 0).astype(F32).astype(BF16)
    return jnp.concatenate([v, extra], axis=1)


def _row_max(scores):
    m = scores[0].max(axis=-1, keepdims=True)
    for s in scores[1:]:
        m = jnp.maximum(m, s.max(axis=-1, keepdims=True))
    return m


def _normalised(r):
    return r[:, :LANES] / r[:, LANES:LANES + 1]


def _softmax_pv(scores, values):
    m = _row_max(scores)
    r = None
    for s, v in zip(scores, values):
        part = _dot(jnp.exp2(s - m).astype(BF16), v)
        r = part if r is None else r + part
    return _normalised(r)


def _lane_halves(q):
    q = q.astype(F32)
    lane = lax.broadcasted_iota(jnp.int32, q.shape, 1)
    return jnp.where(lane < 64, q, 0.0).astype(BF16), jnp.where(lane >= 64, q, 0.0).astype(BF16)


def _merge_halves(lo, hi):
    lane = lax.broadcasted_iota(jnp.int32, lo.shape, 1)
    return jnp.where(lane < 64, lo, hi)


def _at_sequence_start(u, n_steps, fn):
    if n_steps == 1:
        fn()
    else:
        pl.when(u == 0)(fn)


def _da_kernel(*refs, lam_init, need_ctx):
    lam_ref, sub_ref, q0_ref = refs[:3]
    q_refs, refs = refs[3:3 + ATTN_TILES_PER_STEP], refs[3 + ATTN_TILES_PER_STEP:]
    qa_ref = q_refs[0]
    if need_ctx:
        qz_ref, refs = refs[0], refs[1:]
    kx_ref, kz_ref, kxb_ref, kzb_ref, vx_ref, vz_ref, ox_ref = refs[:7]
    refs = refs[7:]
    if need_ctx:
        oz_ref, refs = refs[0], refs[1:]
    p0_ref, p1_ref, va_ref = refs

    u = pl.program_id(2)
    tq = qa_ref.shape[0]
    n_lat = kx_ref.shape[0]
    lp = lam_ref[...]
    lam = (jnp.exp(jnp.sum(lp[0:1, :] * lp[1:2, :], axis=-1, keepdims=True))
           - jnp.exp(jnp.sum(lp[2:3, :] * lp[3:4, :], axis=-1, keepdims=True)) + lam_init)

    def qk_softmax(q_ref, kx, kz, p_ref):
        qs = jnp.concatenate(_lane_halves(q_ref[...]), axis=0)
        sx = _dot_nt(qs, kx[...])
        sz = _dot_nt(qs, kz[...])
        m = _row_max([sx, sz])
        p_ref[:, :n_lat] = jnp.exp2(sx - m).astype(BF16)
        p_ref[:, n_lat:] = jnp.exp2(sz - m).astype(BF16)

    def finish(maps):
        acc = maps[0] - lam * maps[1]
        return (_rms(acc) * sub_ref[...] * (1.0 - lam_init)).astype(BF16)

    def pv(p_ref):
        r = _normalised(_dot(p_ref[...], va_ref[...]))
        return finish([r[:tq], r[tq:]])

    @pl.when(jnp.logical_and(jnp.logical_and(pl.program_id(0) == 0, pl.program_id(1) == 0), u == 0))
    def _():
        qk_softmax(q0_ref, kx_ref, kz_ref, p0_ref)

    def sequence_start():
        va_ref[:n_lat, :] = _with_ones_column(vx_ref[...])
        va_ref[n_lat:, :] = _with_ones_column(vz_ref[...])
        if need_ctx:
            kz = kz_ref[...]
            vz = va_ref[n_lat:, :]
            oz_ref[...] = finish([_softmax_pv([_dot_nt(qm, kz)], [vz]) for qm in _lane_halves(qz_ref[...])])

    _at_sequence_start(u, n_lat // ox_ref.shape[0], sequence_start)

    slots = (p0_ref, p1_ref)
    for j in range(ATTN_TILES_PER_STEP):
        keys = (kxb_ref, kzb_ref) if j == ATTN_TILES_PER_STEP - 1 else (kx_ref, kz_ref)
        qk_softmax(q_refs[j], *keys, slots[(j + 1) % 2])
        ox_ref[j * tq:(j + 1) * tq, :] = pv(slots[j % 2])


def _attn_specs(n_head_blocks, n_batch, lat_len, ctx_len, tq, need_ctx, chain_heads):
    n_lat_tiles = lat_len // tq
    per_step = ATTN_TILES_PER_STEP
    assert per_step % 2 == 0 and n_lat_tiles % per_step == 0
    n_steps = n_lat_tiles // per_step
    ctx_blk0 = n_batch * lat_len // ctx_len

    def ahead(h, b, u):
        if chain_heads:
            s = jnp.minimum(h * n_batch + b + 1, n_head_blocks * n_batch - 1)
            hn, bn = s // n_batch, s % n_batch
        else:
            hn, bn = h, jnp.minimum(b + 1, n_batch - 1)
        last = u == n_steps - 1
        return jnp.where(last, hn, h), jnp.where(last, bn, b), jnp.where(last, 0, per_step * (u + 1))

    def ahead_map(h, b, u):
        hn, bn, tn = ahead(h, b, u)
        return (bn * n_lat_tiles + tn, hn)

    def tile_map(j):
        return lambda h, b, u: (b * n_lat_tiles + per_step * u + j, h)

    q_specs = [pl.BlockSpec((tq, LANES), lambda h, b, u: (0, h))]
    q_specs += [pl.BlockSpec((tq, LANES), tile_map(j)) for j in range(1, per_step)]
    q_specs.append(pl.BlockSpec((tq, LANES), ahead_map))
    if need_ctx:
        q_specs.append(pl.BlockSpec((ctx_len, LANES), lambda h, b, u: (ctx_blk0 + b, h)))
    lat_spec = pl.BlockSpec((lat_len, LANES), lambda h, b, u: (b, h))
    ctx_spec = pl.BlockSpec((ctx_len, LANES), lambda h, b, u: (ctx_blk0 + b, h))
    kv_specs = [
        lat_spec, ctx_spec,
        pl.BlockSpec((lat_len, LANES), lambda h, b, u: (ahead(h, b, u)[1], ahead(h, b, u)[0])),
        pl.BlockSpec((ctx_len, LANES), lambda h, b, u: (ctx_blk0 + ahead(h, b, u)[1], ahead(h, b, u)[0])),
        lat_spec, ctx_spec,
    ]
    out_specs = [pl.BlockSpec((per_step * tq, LANES), lambda h, b, u: (b * n_steps + u, h))]
    if need_ctx:
        out_specs.append(pl.BlockSpec((ctx_len, LANES), lambda h, b, u: (b, h)))
    return q_specs, kv_specs, out_specs


def _da_call(q, k, v, lam_p, subln, lam_init, n_batch, lat_len, ctx_len, need_ctx):
    d = q.shape[1]
    tq = ctx_len
    q_specs, kv_specs, out_specs = _attn_specs(DA_HEADS, n_batch, lat_len, ctx_len, tq, need_ctx, True)
    out_shape = [jax.ShapeDtypeStruct((n_batch * lat_len, d), BF16)]
    if need_ctx:
        out_shape.append(jax.ShapeDtypeStruct((n_batch * ctx_len, d), BF16))
    n_q = len(q_specs)
    return pl.pallas_call(
        functools.partial(_da_kernel, lam_init=lam_init, need_ctx=need_ctx),
        grid=(DA_HEADS, n_batch, lat_len // (ATTN_TILES_PER_STEP * tq)),
        in_specs=[pl.BlockSpec(lam_p.shape, lambda h, b, u: (0, 0)),
                  pl.BlockSpec((1, LANES), lambda h, b, u: (0, 0))] + q_specs + kv_specs,
        out_specs=out_specs,
        out_shape=out_shape,
        scratch_shapes=[pltpu.VMEM((2 * tq, lat_len + ctx_len), BF16),
                        pltpu.VMEM((2 * tq, lat_len + ctx_len), BF16),
                        pltpu.VMEM((lat_len + ctx_len, 2 * LANES), BF16)],
        compiler_params=_cparams(3, ordered=True),
        name="diff_attn",
    )(lam_p, subln.reshape(1, LANES), *([q] * n_q), k, k, k, k, v, v)


def _ret_kernel(dec_ref, qx_ref, qz_ref, kx_ref, kz_ref, vx_ref, vz_ref, gx_ref, gz_ref, ox_ref, oz_ref,
                acc_ref, dsb_ref):
    c = RET_CHUNK
    n_chunks = qx_ref.shape[0] // c
    head = pl.program_id(1)
    lgs = jax.nn.log_sigmoid(dec_ref[...])
    sel = lax.broadcasted_iota(jnp.int32, lgs.shape, 1) == head
    lgs = jnp.sum(jnp.where(sel, lgs, 0.0), axis=-1, keepdims=True)
    lg_f = lgs[0:1, :]
    lg_b = lgs[1:2, :]

    ii = lax.broadcasted_iota(jnp.int32, (c, c), 0)
    jj = lax.broadcasted_iota(jnp.int32, (c, c), 1)
    diff = (ii - jj).astype(F32)
    dmat = jnp.where(diff >= 0.0, jnp.exp(jnp.maximum(diff, 0.0) * lg_f), jnp.exp(jnp.maximum(-diff, 0.0) * lg_b))
    idx = lax.broadcasted_iota(jnp.int32, (c, 1), 0).astype(F32)
    xi_f = jnp.exp((idx + 1.0) * lg_f)
    xi_b = jnp.exp((c - idx) * lg_b)
    zeta_f = jnp.exp((c - 1.0 - idx) * lg_f)
    zeta_b = jnp.exp(idx * lg_b)
    decay_f = jnp.exp(c * lg_f)
    decay_b = jnp.exp(c * lg_b)

    def intra(q, k, v):
        return _dot((_dot_nt(q, k) * dmat).astype(BF16), v)

    def state_updates(k, v):
        k32 = k.astype(F32)
        return _dot_tn((k32 * zeta_f).astype(BF16), v), _dot_tn((k32 * zeta_b).astype(BF16), v)

    def finish(o, gate):
        return (_rms(o) * gate.astype(F32)).astype(BF16)

    qz, kz, vz = qz_ref[...], kz_ref[...], vz_ref[...]
    oz_ref[...] = finish(intra(qz, kz, vz), gz_ref[...])
    s_f, s_b = state_updates(kz, vz)

    for i in range(n_chunks):
        rows = slice(i * c, (i + 1) * c)
        q, k, v = qx_ref[rows, :], kx_ref[rows, :], vx_ref[rows, :]
        acc_ref[rows, :] = intra(q, k, v) + xi_f * _dot(q, s_f.astype(BF16))
        d_f, d_b = state_updates(k, v)
        dsb_ref[i] = d_b
        s_f = decay_f * s_f + d_f

    for i in reversed(range(n_chunks)):
        rows = slice(i * c, (i + 1) * c)
        o = acc_ref[rows, :] + xi_b * _dot(qx_ref[rows, :], s_b.astype(BF16))
        ox_ref[rows, :] = finish(o, gx_ref[rows, :])
        s_b = decay_b * s_b + dsb_ref[i]


def _ret_call(q, k, v, sg, decay, n_batch, lat_len, ctx_len):
    dq = q.shape[1]
    dv = v.shape[1]
    dkh = dq // RET_HEADS
    dvh = dv // RET_HEADS
    ctx_blk0 = n_batch * lat_len // ctx_len

    def lat(b, h):
        return (b, h)

    def ctx(b, h):
        return (ctx_blk0 + b, h)

    return pl.pallas_call(
        _ret_kernel,
        grid=(n_batch, RET_HEADS),
        in_specs=[
            pl.BlockSpec(decay.shape, lambda b, h: (0, 0)),
            pl.BlockSpec((lat_len, dkh), lat), pl.BlockSpec((ctx_len, dkh), ctx),
            pl.BlockSpec((lat_len, dkh), lat), pl.BlockSpec((ctx_len, dkh), ctx),
            pl.BlockSpec((lat_len, dvh), lat), pl.BlockSpec((ctx_len, dvh), ctx),
            pl.BlockSpec((lat_len, dvh), lat), pl.BlockSpec((ctx_len, dvh), ctx),
        ],
        out_specs=[pl.BlockSpec((lat_len, dvh), lat), pl.BlockSpec((ctx_len, dvh), lambda b, h: (b, h))],
        out_shape=[jax.ShapeDtypeStruct((n_batch * lat_len, dv), BF16),
                   jax.ShapeDtypeStruct((n_batch * ctx_len, dv), BF16)],
        scratch_shapes=[pltpu.VMEM((lat_len, dvh), F32),
                        pltpu.VMEM((lat_len // RET_CHUNK, dkh, dvh), F32)],
        compiler_params=_cparams(2),
        name="retention",
    )(decay, q, q, k, k, v, v, sg, sg)


def _na_tile_plan(n_rows, rows_per_tile):
    kh = min(NA_KH, n_rows)
    span = kh + rows_per_tile
    classes, tile_class = [], []
    for t in range(n_rows // rows_per_tile):
        r0 = t * rows_per_tile
        start = min(max(r0 - kh // 2, 0), n_rows - span)
        sig = []
        for r in range(r0, r0 + rows_per_tile):
            rs = min(max(r - kh // 2, 0), n_rows - kh)
            assert start <= rs and rs + kh <= start + span
            sig.append((rs - start, start - r + NA_KH - 1))
        sig = tuple(sig)
        if sig not in classes:
            classes.append(sig)
        tile_class.append(classes.index(sig))
    return tile_class, classes


def _na_kernel(rpb_ref, q0_ref, *refs, n_rows):
    per_step = ATTN_TILES_PER_STEP
    q_refs, refs = refs[:per_step], refs[per_step:]
    (qz_ref, kx_ref, kz_ref, kxb_ref, kzb_ref, vx_ref, vz_ref,
     ox_ref, oz_ref, tdup_ref, bias_ref, p0_ref, p1_ref, vxa_ref, vza_ref) = refs
    qa_ref = q_refs[0]
    pair = pl.program_id(0)
    b = pl.program_id(1)
    u = pl.program_id(2)
    w = GRID_W
    tq = qa_ref.shape[0]
    rows_per_tile = tq // w
    n_lat_tiles = n_rows // rows_per_tile
    kh = min(NA_KH, n_rows)
    span = kh + rows_per_tile
    n_win = span * w
    tile_class, classes = _na_tile_plan(n_rows, rows_per_tile)
    n_dr = 2 * NA_KH - 1
    n_dc = 2 * NA_KW - 1
    neg_inf = jnp.full((w, LANES), -jnp.inf, F32)

    @pl.when(jnp.logical_and(b == 0, u == 0))
    def _():
        col_q = lax.broadcasted_iota(jnp.int32, (w, LANES), 0)
        lane = lax.broadcasted_iota(jnp.int32, (w, LANES), 1)
        col_k = lane % w
        dc = col_k - col_q + (NA_KW - 1)
        cs = jnp.clip(col_q - NA_KW // 2, 0, w - NA_KW)
        in_cols = jnp.logical_and(col_k >= cs, col_k < cs + NA_KW)
        for hh in range(2):
            for dr in range(n_dr):
                base = ((pair * 2 + hh) * n_dr + dr) * n_dc
                toe = lax.fori_loop(0, n_dc, lambda i, acc: jnp.where(dc == i, rpb_ref[base + i], acc), neg_inf)
                tdup_ref[hh, dr] = jnp.where(in_cols, toe * LOG2E, -jnp.inf)
        for hh in range(2):
            for ci, sig in enumerate(classes):
                for rr, (first, dr0) in enumerate(sig):
                    for j in range(n_win // LANES):
                        halves = []
                        for a in (2 * j, 2 * j + 1):
                            ok = first <= a < first + kh
                            halves.append(tdup_ref[hh, dr0 + a] if ok else neg_inf)
                        bias_ref[hh, ci, rr * w:(rr + 1) * w, j * LANES:(j + 1) * LANES] = (
                            jnp.where(lane < w, halves[0], halves[1]))

    def window_start(t):
        if isinstance(t, int):
            return min(max(t * rows_per_tile - kh // 2, 0), n_rows - span) * w
        start_row = jnp.clip(t * rows_per_tile - kh // 2, 0, n_rows - span)
        return pl.multiple_of(start_row * w, w)

    def tile_class_of(t):
        if isinstance(t, int):
            return tile_class[t]
        cls = jnp.int32(tile_class[0])
        for tt in range(1, n_lat_tiles):
            cls = jnp.where(t == tt, tile_class[tt], cls)
        return cls

    def qk_softmax(q_ref, kx, kz, t, p_ref):
        kwin = kx[pl.ds(window_start(t), n_win), :]
        kz = kz[...]
        cls = tile_class_of(t)
        for hh, qm in enumerate(_lane_halves(q_ref[...])):
            s_lat = _dot_nt(qm, kwin) + bias_ref[hh, cls]
            s_ctx = _dot_nt(qm, kz)
            m = _row_max([s_lat, s_ctx])
            p_ref[hh, :, :n_win] = jnp.exp2(s_lat - m).astype(BF16)
            p_ref[hh, :, n_win:] = jnp.exp2(s_ctx - m).astype(BF16)

    def pv(t, p_ref):
        vwin = vxa_ref[pl.ds(window_start(t), n_win), :]
        vz = vza_ref[...]
        outs = [_normalised(_dot(p_ref[hh, :, :n_win], vwin) + _dot(p_ref[hh, :, n_win:], vz)) for hh in range(2)]
        return _merge_halves(*outs).astype(BF16)

    @pl.when(jnp.logical_and(b == 0, u == 0))
    def _():
        qk_softmax(q0_ref, kx_ref, kz_ref, 0, p0_ref)

    def sequence_start():
        vxa_ref[...] = _with_ones_column(vx_ref[...])
        vza_ref[...] = _with_ones_column(vz_ref[...])
        kz = kz_ref[...]
        vz = vza_ref[...]
        outs = [_softmax_pv([_dot_nt(qm, kz)], [vz]) for qm in _lane_halves(qz_ref[...])]
        oz_ref[...] = _merge_halves(*outs).astype(BF16)

    _at_sequence_start(u, n_lat_tiles // per_step, sequence_start)

    slots = (p0_ref, p1_ref)
    for j in range(per_step):
        t = per_step * u + j
        if j == per_step - 1:
            t_next = jnp.where(u == n_lat_tiles // per_step - 1, 0, t + 1)
            qk_softmax(q_refs[j], kxb_ref, kzb_ref, t_next, slots[(j + 1) % 2])
        else:
            qk_softmax(q_refs[j], kx_ref, kz_ref, t + 1, slots[(j + 1) % 2])
        ox_ref[j * tq:(j + 1) * tq, :] = pv(t, slots[j % 2])


def _na_call(q, k, v, rpb, n_batch, lat_len, ctx_len):
    d = q.shape[1]
    tq = ctx_len
    assert tq % GRID_W == 0
    n_rows = lat_len // GRID_W
    rows_per_tile = tq // GRID_W
    n_win = (min(NA_KH, n_rows) + rows_per_tile) * GRID_W
    n_classes = len(_na_tile_plan(n_rows, rows_per_tile)[1])
    q_specs, kv_specs, out_specs = _attn_specs(NA_HEADS // 2, n_batch, lat_len, ctx_len, tq, True, False)
    return pl.pallas_call(
        functools.partial(_na_kernel, n_rows=n_rows),
        grid=(NA_HEADS // 2, n_batch, lat_len // (ATTN_TILES_PER_STEP * tq)),
        in_specs=[pl.BlockSpec(memory_space=pltpu.SMEM)] + q_specs + kv_specs,
        out_specs=out_specs,
        out_shape=[jax.ShapeDtypeStruct((n_batch * lat_len, d), BF16),
                   jax.ShapeDtypeStruct((n_batch * ctx_len, d), BF16)],
        scratch_shapes=[pltpu.VMEM((2, 2 * NA_KH - 1, GRID_W, LANES), F32),
                        pltpu.VMEM((2, n_classes, tq, n_win), F32),
                        pltpu.VMEM((2, tq, n_win + ctx_len), BF16),
                        pltpu.VMEM((2, tq, n_win + ctx_len), BF16),
                        pltpu.VMEM((lat_len, 2 * LANES), BF16),
                        pltpu.VMEM((ctx_len, 2 * LANES), BF16)],
        compiler_params=_cparams(3, ordered=True),
        name="nbr_attn",
    )(rpb.astype(F32).reshape(-1), *([q] * len(q_specs)), k, k, k, k, v, v)


def _rope_tables(lat_len, head_dim, tm):
    f = head_dim // 4
    t = jnp.arange(lat_len)
    inv = ROPE_BASE ** (-jnp.arange(f, dtype=F32) / f)
    ang = jnp.concatenate([(t // GRID_W).astype(F32)[:, None] * inv, (t % GRID_W).astype(F32)[:, None] * inv], axis=-1)
    cos, sin = jnp.cos(ang), jnp.sin(ang)
    if head_dim // 2 < LANES:
        reps = LANES // head_dim
        cos = jnp.tile(jnp.concatenate([cos, cos], axis=-1), (1, reps))
        sin = jnp.tile(jnp.concatenate([-sin, sin], axis=-1), (1, reps))
    cos = jnp.concatenate([cos, jnp.ones((tm, LANES), F32)], axis=0)
    sin = jnp.concatenate([sin, jnp.zeros((tm, LANES), F32)], axis=0)
    return cos, sin


def kernel(x, c, ctx, c_ctx, w_ada, b_ada, norm_g, ffn_in, ffn_out, final_g, da_w_qkv, da_w_o, da_lambda, da_subln,
           ret_w_in, ret_w_o, ret_decay, na_w_qkv, na_w_o, na_rpb):
    n_batch, lat_len, d = x.shape
    ctx_len = ctx.shape[1]
    assert n_batch + 1 <= MOD_ROWS
    rows = _Rows(n_batch * lat_len, lat_len, n_batch * ctx_len, tm=512)

    cond = jnp.concatenate([c, c_ctx[None, :], jnp.zeros((MOD_ROWS - n_batch - 1, d), F32)], axis=0)
    mods = _ada_call(cond, w_ada, b_ada).reshape(DEPTH, MOD_ROWS, 6, d)
    norm_g2 = norm_g.reshape(2 * DEPTH, 1, d)
    streams = (x.reshape(rows.n_lat, d), ctx.reshape(rows.n_ctx, d))

    rope64 = _rope_tables(lat_len, d // (2 * DA_HEADS), rows.tm)
    rope256 = _rope_tables(lat_len, d // RET_HEADS, rows.tm)

    da_w_o, ret_w_in, ret_w_o, na_w_o, ffn_in, ffn_out = (
        w.astype(BF16) for w in (da_w_o, ret_w_in, ret_w_o, na_w_o, ffn_in, ffn_out))

    ia = ib = ic = 0
    for li in range(DEPTH):
        kind = li % 3
        final = li == DEPTH - 1
        if kind == 0:
            q, k, v = _pre_call(_pre_da_kernel, streams, mods, norm_g2, li, da_w_qkv, ia, rope64,
                                (d, d, d), rows, "pre_diff_attn")
            lam_init = 0.8 - 0.6 * math.exp(-0.3 * li)
            a_parts = _da_call(q, k, v, da_lambda[ia], da_subln[ia], lam_init, n_batch, lat_len, ctx_len, not final)
            w_o, w_o_layer = da_w_o, ia
            ia += 1
        elif kind == 1:
            q, k, v, sg = _pre_call(_pre_ret_kernel, streams, mods, norm_g2, li, ret_w_in, ib, rope256,
                                    (d, d, 2 * d, 2 * d), rows, "pre_retention")
            a_parts = _ret_call(q, k, v, sg, ret_decay[ib], n_batch, lat_len, ctx_len)
            w_o, w_o_layer = ret_w_o, ib
            ib += 1
        else:
            q, k, v = _pre_call(_pre_na_kernel, streams, mods, norm_g2, li, na_w_qkv, ic, None,
                                (d, d, d), rows, "pre_nbr_attn")
            a_parts = _na_call(q, k, v, na_rpb[ic], n_batch, lat_len, ctx_len)
            w_o, w_o_layer = na_w_o, ic
            ic += 1
        streams = (_post_call(a_parts, streams, mods, norm_g2, li, w_o, w_o_layer, ffn_in, ffn_out, final_g, rows,
                              final),)
    return streams[0].reshape(n_batch, lat_len, d)
```
